```python
import jax, jax.numpy as jnp
from jax import lax
import numpy as np

D_MODEL = 4096
BATCH = 4
SEQ = 4096
DEPTH = 2

N_A = DEPTH // 2
N_B = DEPTH - N_A

RET_HEADS = 16
RET_HEAD_DIM = D_MODEL // RET_HEADS
RET_CHUNK = 128

NSA_HEADS = 32
NSA_HEAD_DIM = D_MODEL // NSA_HEADS
NSA_KV_GROUPS = 4
NSA_HPG = NSA_HEADS // NSA_KV_GROUPS
CMP_BLOCK = 32
CMP_STRIDE = 16
SEL_BLOCK = 64
N_SELECT = 16
WINDOW = 512
NSA_Q_BLOCK = 16
KV_WIDTH = NSA_KV_GROUPS * NSA_HEAD_DIM

D_FF = ((8 * D_MODEL // 3 + 255) // 256) * 256
CONV_WIDTH = 3

NORM_EPS = 1e-6
GN_EPS = 1e-5
NEG_BIG = -1e30
SEL_FORCE = 1e9

kernel_name = "hybrid_retention_nsa_yoco_convffn"

f32 = jnp.float32


def rmsnorm(x, g):
    xf = x.astype(f32)
    y = xf * lax.rsqrt(jnp.mean(xf * xf, axis=-1, keepdims=True) + NORM_EPS)
    return (y * g.astype(f32)).astype(x.dtype)


def masked_softmax(s, mask):
    s = jnp.where(mask, s, NEG_BIG)
    m = jnp.max(s, axis=-1, keepdims=True)
    e = jnp.where(mask, jnp.exp(s - m), 0.0)
    return e / jnp.maximum(jnp.sum(e, axis=-1, keepdims=True), 1e-30)


def alibi_slopes():
    h = jnp.arange(1, NSA_HEADS + 1, dtype=f32)
    return jnp.exp2(-8.0 * h / NSA_HEADS).reshape(NSA_KV_GROUPS, NSA_HPG)


def retention(h, w_in, gn_gain, w_out):
    B, S, _ = h.shape
    nc = S // RET_CHUNK
    proj = h @ w_in
    q, k, v, g = jnp.split(proj, 4, axis=-1)

    def chunks(t):
        return t.reshape(B, nc, RET_CHUNK, RET_HEADS, RET_HEAD_DIM).transpose(1, 0, 3, 2, 4).astype(f32)

    qc, kc, vc = chunks(q), chunks(k) * (RET_HEAD_DIM ** -0.5), chunks(v)
    log_g = jnp.log1p(-jnp.exp2(-5.0 - jnp.arange(RET_HEADS, dtype=f32)))
    i = jnp.arange(RET_CHUNK, dtype=f32)
    diff = i[:, None] - i[None, :]
    decay_mask = jnp.where(diff >= 0, jnp.exp(jnp.maximum(diff, 0.0)[None] * log_g[:, None, None]), 0.0)
    q_decay = jnp.exp((i + 1.0)[None] * log_g[:, None])
    k_decay = jnp.exp((RET_CHUNK - 1.0 - i)[None] * log_g[:, None])
    chunk_decay = jnp.exp(RET_CHUNK * log_g)

    def step(state, inp):
        qi, ki, vi = inp
        qk = jnp.einsum('bhid,bhjd->bhij', qi, ki) * decay_mask
        inner = jnp.einsum('bhij,bhjd->bhid', qk, vi)
        cross = jnp.einsum('bhid,bhde->bhie', qi, state) * q_decay[:, :, None]
        state = state * chunk_decay[:, None, None] + jnp.einsum(
            'bhjd,bhje->bhde', ki * k_decay[:, :, None], vi)
        return state, inner + cross

    state0 = jnp.zeros((B, RET_HEADS, RET_HEAD_DIM, RET_HEAD_DIM), f32)
    _, y = lax.scan(step, state0, (qc, kc, vc))
    y = y.transpose(1, 0, 3, 2, 4).reshape(B, S, RET_HEADS, RET_HEAD_DIM)
    mu = jnp.mean(y, axis=-1, keepdims=True)
    var = jnp.mean(jnp.square(y - mu), axis=-1, keepdims=True)
    y = ((y - mu) * lax.rsqrt(var + GN_EPS)).reshape(B, S, D_MODEL) * gn_gain.astype(f32)
    return (jax.nn.silu(g.astype(f32)) * y).astype(h.dtype) @ w_out


def compress(tok, pos_emb, w1, w2):
    B, S = tok.shape[0], tok.shape[1]
    nc = (S - CMP_BLOCK) // CMP_STRIDE + 1
    idx = jnp.arange(nc)[:, None] * CMP_STRIDE + jnp.arange(CMP_BLOCK)[None, :]
    blk = tok[:, idx] + pos_emb[None, None, :, None, :]
    blk = blk.transpose(0, 1, 3, 2, 4).reshape(B, nc, NSA_KV_GROUPS, CMP_BLOCK * NSA_HEAD_DIM)
    return jax.nn.silu(blk @ w1) @ w2


def shared_kv(hkv, w_kv, cmp_pos, cmp_w1, cmp_w2):
    B, S, _ = hkv.shape
    kv = (hkv @ w_kv).reshape(B, S, 6, NSA_KV_GROUPS, NSA_HEAD_DIM)
    k_cmp_tok, v_cmp_tok = kv[:, :, 0], kv[:, :, 1]
    k_slc, v_slc = kv[:, :, 2], kv[:, :, 3]
    k_win, v_win = kv[:, :, 4], kv[:, :, 5]
    kc = compress(k_cmp_tok, cmp_pos[0], cmp_w1[0], cmp_w2[0])
    vc = compress(v_cmp_tok, cmp_pos[1], cmp_w1[1], cmp_w2[1])
    ns = S // SEL_BLOCK

    def to_blocks(t):
        return t.transpose(0, 2, 1, 3).reshape(B, NSA_KV_GROUPS, ns, SEL_BLOCK, NSA_HEAD_DIM)

    pad = ((0, 0), (WINDOW, 0), (0, 0), (0, 0))
    return (kc, vc, to_blocks(k_slc), to_blocks(v_slc), jnp.pad(k_win, pad), jnp.pad(v_win, pad))


def selection_map(nc, ns):
    start = jnp.arange(nc)[:, None] * CMP_STRIDE
    j = jnp.arange(ns)[None, :]
    ov = jnp.minimum(start + CMP_BLOCK, (j + 1) * SEL_BLOCK) - jnp.maximum(start, j * SEL_BLOCK)
    return jnp.clip(ov, 0).astype(f32) / CMP_STRIDE


def nsa(h, shared, w_q, w_out):
    B, S, _ = h.shape
    kc, vc, ks_blocks, vs_blocks, kw_pad, vw_pad = shared
    nc = kc.shape[1]
    ns = ks_blocks.shape[2]
    n_sel = min(N_SELECT, ns)
    scale = NSA_HEAD_DIM ** -0.5
    slopes = alibi_slopes()
    sel_map = selection_map(nc, ns)
    pos_cmp = jnp.arange(nc) * CMP_STRIDE + CMP_BLOCK - 1
    gather = jax.vmap(jax.vmap(lambda blocks, i: blocks[i]))

    proj = h @ w_q
    q = proj[..., :D_MODEL].reshape(B, S, NSA_KV_GROUPS, NSA_HPG, NSA_HEAD_DIM)
    gate = jax.nn.sigmoid(proj[..., D_MODEL:].astype(f32)).reshape(B, S, NSA_KV_GROUPS, NSA_HPG, 3)
    nqb = S // NSA_Q_BLOCK
    q_blocks = jnp.moveaxis(q.reshape(B, nqb, NSA_Q_BLOCK, NSA_KV_GROUPS, NSA_HPG, NSA_HEAD_DIM), 1, 0)
    g_blocks = jnp.moveaxis(gate.reshape(B, nqb, NSA_Q_BLOCK, NSA_KV_GROUPS, NSA_HPG, 3), 1, 0)
    starts = jnp.arange(nqb, dtype=jnp.int32) * NSA_Q_BLOCK

    def block(args):
        qb, gb, s0 = args
        t = s0 + jnp.arange(NSA_Q_BLOCK, dtype=jnp.int32)
        dist_c = t[:, None] - pos_cmp[None, :]
        s_c = jnp.einsum('bqghd,bngd->bghqn', qb, kc).astype(f32) * scale
        s_c = s_c - slopes[None, :, :, None, None] * dist_c.astype(f32)
        p_c = masked_softmax(s_c, (dist_c >= 0)[None, None, None])
        o_c = jnp.einsum('bghqn,bngd->bqghd', p_c, vc.astype(f32))
        imp = jnp.einsum('bghqn,nj->bgqj', p_c, sel_map)
        j = jnp.arange(ns)
        cur = t // SEL_BLOCK
        valid = (j[None, :] * SEL_BLOCK) <= t[:, None]
        forced = (j[None, :] == 0) | (j[None, :] == cur[:, None]) | (j[None, :] == cur[:, None] - 1)
        score = jnp.where(valid, jnp.where(forced, SEL_FORCE, imp), -SEL_FORCE)
        _, idx = lax.top_k(score, n_sel)
        k_sel = gather(ks_blocks, idx)
        v_sel = gather(vs_blocks, idx)
        pos = idx[..., None] * SEL_BLOCK + jnp.arange(SEL_BLOCK)
        dist_s = (t[None, None, :, None, None] - pos)[:, :, None]
        s_s = jnp.einsum('bqghd,bgqnld->bghqnl', qb, k_sel).astype(f32) * scale
        s_s = s_s - slopes[None, :, :, None, None, None] * dist_s.astype(f32)
        Bq = s_s.shape[0]
        flat = (Bq, NSA_KV_GROUPS, NSA_HPG, NSA_Q_BLOCK, n_sel * SEL_BLOCK)
        mask_s = jnp.broadcast_to(dist_s >= 0, s_s.shape).reshape(flat)
        p_s = masked_softmax(s_s.reshape(flat), mask_s).reshape(s_s.shape)
        o_s = jnp.einsum('bghqnl,bgqnld->bqghd', p_s, v_sel.astype(f32))
        kw = lax.dynamic_slice_in_dim(kw_pad, s0, NSA_Q_BLOCK + WINDOW, axis=1)
        vw = lax.dynamic_slice_in_dim(vw_pad, s0, NSA_Q_BLOCK + WINDOW, axis=1)
        kpos = s0 - WINDOW + jnp.arange(NSA_Q_BLOCK + WINDOW, dtype=jnp.int32)
        dist_w = t[:, None] - kpos[None, :]
        mask_w = (dist_w >= 0) & (dist_w < WINDOW) & (kpos[None, :] >= 0)
        s_w = jnp.einsum('bqghd,bkgd->bghqk', qb, kw).astype(f32) * scale
        s_w = s_w - slopes[None, :, :, None, None] * dist_w.astype(f32)
        p_w = masked_softmax(s_w, mask_w[None, None, None])
        o_w = jnp.einsum('bghqk,bkgd->bqghd', p_w, vw.astype(f32))
        o = gb[..., 0:1] * o_c + gb[..., 1:2] * o_s + gb[..., 2:3] * o_w
        return o.astype(h.dtype)

    out = lax.map(block, (q_blocks, g_blocks, starts))
    out = jnp.moveaxis(out, 0, 1).reshape(B, S, D_MODEL)
    return out @ w_out


def conv_ffn(h, w_in, conv_w, conv_b, w_out):
    a, u = jnp.split(h @ w_in, 2, axis=-1)
    a = lax.conv_general_dilated(
        a, conv_w[:, None, :].astype(a.dtype), window_strides=(1,),
        padding=[(CONV_WIDTH - 1, 0)], dimension_numbers=('NWC', 'WIO', 'NWC'),
        feature_group_count=a.shape[-1]) + conv_b
    return (jax.nn.silu(a) * u) @ w_out


def setup_inputs(seed: int = 0) -> dict:
    key = jax.random.key(seed)
    ks = jax.random.split(key, 20)
    nrm = jax.random.normal
    D, F = D_MODEL, D_FF
    return {
        "x": nrm(ks[0], (BATCH, SEQ, D), f32),
        "attn_norm": 1.0 + 0.02 * nrm(ks[1], (DEPTH, D), f32),
        "ffn_norm": 1.0 + 0.02 * nrm(ks[2], (DEPTH, D), f32),
        "w_ret_in": nrm(ks[3], (N_A, D, 4 * D), f32) * D ** -0.5,
        "ret_gn_gain": 1.0 + 0.02 * nrm(ks[4], (N_A, D), f32),
        "w_ret_out": nrm(ks[5], (N_A, D, D), f32) * D ** -0.5,
        "kv_norm": 1.0 + 0.02 * nrm(ks[6], (D,), f32),
        "w_kv": nrm(ks[7], (D, 6 * KV_WIDTH), f32) * D ** -0.5,
        "cmp_pos": 0.02 * nrm(ks[8], (2, CMP_BLOCK, NSA_HEAD_DIM), f32),
        "cmp_w1": nrm(ks[9], (2, CMP_BLOCK * NSA_HEAD_DIM, NSA_HEAD_DIM), f32) * (CMP_BLOCK * NSA_HEAD_DIM) ** -0.5,
        "cmp_w2": nrm(ks[10], (2, NSA_HEAD_DIM, NSA_HEAD_DIM), f32) * NSA_HEAD_DIM ** -0.5,
        "w_nsa_q": nrm(ks[11], (N_B, D, D + 3 * NSA_HEADS), f32) * D ** -0.5,
        "w_nsa_out": nrm(ks[12], (N_B, D, D), f32) * D ** -0.5,
        "w_ffn_in": nrm(ks[13], (DEPTH, D, 2 * F), f32) * D ** -0.5,
        "conv_w": nrm(ks[14], (DEPTH, CONV_WIDTH, F), f32) * CONV_WIDTH ** -0.5,
        "conv_b": 0.02 * nrm(ks[15], (DEPTH, F), f32),
        "w_ffn_out": nrm(ks[16], (DEPTH, F, D), f32) * F ** -0.5,
        "final_norm": 1.0 + 0.02 * nrm(ks[17], (D,), f32),
    }


def reference(x, attn_norm, ffn_norm, w_ret_in, ret_gn_gain, w_ret_out, kv_norm, w_kv,
              cmp_pos, cmp_w1, cmp_w2, w_nsa_q, w_nsa_out, w_ffn_in, conv_w, conv_b,
              w_ffn_out, final_norm):
    shared = None
    for layer in range(DEPTH):
        if layer < N_A:
            x = x + retention(rmsnorm(x, attn_norm[layer]), w_ret_in[layer],
                              ret_gn_gain[layer], w_ret_out[layer])
        else:
            if layer == N_A:
                shared = shared_kv(rmsnorm(x, kv_norm), w_kv, cmp_pos, cmp_w1, cmp_w2)
            b = layer - N_A
            x = x + nsa(rmsnorm(x, attn_norm[layer]), shared, w_nsa_q[b], w_nsa_out[b])
        x = x + conv_ffn(rmsnorm(x, ffn_norm[layer]), w_ffn_in[layer], conv_w[layer],
                         conv_b[layer], w_ffn_out[layer])
    return rmsnorm(x, final_norm)
```

```python
import functools

import jax
import jax.numpy as jnp
from jax import lax
from jax.experimental import pallas as pl
from jax.experimental.pallas import tpu as pltpu

f32 = jnp.float32
bf16 = jnp.bfloat16

RET_HEAD_DIM = 256
RET_CHUNK = 128
NSA_HEAD_DIM = 128
NSA_KV_GROUPS = 4
CMP_BLOCK = 32
CMP_STRIDE = 16
SEL_BLOCK = 64
SEL_SHIFT = 6
N_SELECT = 16
WINDOW = 512
CONV_WIDTH = 3
NORM_EPS = 1e-6
GN_EPS = 1e-5
NEG_BIG = -1e30

V7X_VMEM_BYTES = 64 * 1024 * 1024
VMEM_LIMIT = V7X_VMEM_BYTES - 8 * 1024 * 1024
LANES = 128
BF16_SUBLANES = 16


def _pick(n, candidates):
    for c in candidates:
        if c <= n and n % c == 0:
            return c
    return n


def _params(sem):
    return pltpu.CompilerParams(dimension_semantics=sem, vmem_limit_bytes=VMEM_LIMIT)


def _dot(a, b):
    return jnp.dot(a, b, preferred_element_type=f32)


def _dot_nt(a, b):
    return lax.dot_general(a, b, (((1,), (1,)), ((), ())), preferred_element_type=f32)


def _rmsnorm_kernel(x_ref, g_ref, *o_refs):
    x = x_ref[...]
    y = x * lax.rsqrt(jnp.mean(x * x, axis=-1, keepdims=True) + NORM_EPS)
    for n, o_ref in enumerate(o_refs):
        o_ref[...] = (y * g_ref[n:n + 1, :]).astype(o_ref.dtype)


def rmsnorm(x, gains, out_dtype):
    m, d = x.shape
    n = gains.shape[0]
    tm = _pick(m, (256, 128, 64, 32, 16, 8))
    outs = pl.pallas_call(
        _rmsnorm_kernel,
        out_shape=[jax.ShapeDtypeStruct((m, d), out_dtype)] * n,
        grid=(m // tm,),
        in_specs=[pl.BlockSpec((tm, d), lambda i: (i, 0)),
                  pl.BlockSpec((n, d), lambda i: (0, 0))],
        out_specs=[pl.BlockSpec((tm, d), lambda i: (i, 0))] * n,
        compiler_params=_params(("parallel",)),
        name="rmsnorm",
    )(x, gains)
    return outs


def _matmul_kernel(a_ref, w_ref, *rest, nk, has_res, sigmoid):
    if has_res:
        res_ref, rest = rest[0], rest[1:]
    o_ref = rest[0]

    def finish(r):
        if has_res:
            r = r + res_ref[...]
        if sigmoid:
            r = jax.nn.sigmoid(r)
        o_ref[...] = r.astype(o_ref.dtype)

    if nk == 1:
        finish(_dot(a_ref[...], w_ref[...]))
        return
    acc_ref = rest[1]
    k = pl.program_id(2)

    @pl.when(k == 0)
    def _():
        acc_ref[...] = jnp.zeros_like(acc_ref)

    acc_ref[...] += _dot(a_ref[...], w_ref[...])

    @pl.when(k == nk - 1)
    def _():
        finish(acc_ref[...])


def matmul(a, w, res=None, out_dtype=bf16, sigmoid=False, tm=None, tn=None, tk=None, name="matmul"):
    m, kdim = a.shape
    n = w.shape[1]
    tm = tm or _pick(m, (1024, 512, 256, 128))
    tn = tn or _pick(n, (512, 256, 128))
    tk = tk or (kdim if kdim <= 4096 else _pick(kdim, (kdim // 2,)))
    nk = kdim // tk
    in_specs = [pl.BlockSpec((tm, tk), lambda i, j, k: (i, k)),
                pl.BlockSpec((tk, tn), lambda i, j, k: (k, j))]
    args = [a, w]
    if res is not None:
        in_specs.append(pl.BlockSpec((tm, tn), lambda i, j, k: (i, j)))
        args.append(res)
    return pl.pallas_call(
        functools.partial(_matmul_kernel, nk=nk, has_res=res is not None, sigmoid=sigmoid),
        out_shape=jax.ShapeDtypeStruct((m, n), out_dtype),
        grid=(m // tm, n // tn, nk),
        in_specs=in_specs,
        out_specs=pl.BlockSpec((tm, tn), lambda i, j, k: (i, j)),
        scratch_shapes=[] if nk == 1 else [pltpu.VMEM((tm, tn), f32)],
        compiler_params=_params(("parallel", "parallel", "arbitrary")),
        name=name,
    )(*args)


def _retention_kernel(lg_ref, q_ref, k_ref, v_ref, g_ref, gain_ref, o_ref, state_ref):
    c = pl.program_id(2)
    C, dh = q_ref.shape

    @pl.when(c == 0)
    def _():
        state_ref[...] = jnp.zeros_like(state_ref)

    lg = lg_ref[...]
    lg_c = lg[:, :C]
    i_row = lax.broadcasted_iota(jnp.int32, (C, C), 0)
    i_col = lax.broadcasted_iota(jnp.int32, (C, C), 1)
    diff = (i_row - i_col).astype(f32)
    decay_mask = jnp.where(diff >= 0, jnp.exp(jnp.maximum(diff, 0.0) * lg_c), 0.0)
    pos = lax.broadcasted_iota(jnp.int32, (C, dh), 0).astype(f32)
    q_decay = jnp.exp((pos + 1.0) * lg)
    k_decay = jnp.exp((C - 1.0 - pos) * lg)
    chunk_decay = jnp.exp(C * lg)

    q = q_ref[...]
    ks = k_ref[...].astype(f32) * (dh ** -0.5)
    v = v_ref[...]
    state = state_ref[...]
    qk = _dot_nt(q, ks.astype(bf16)) * decay_mask
    inner = _dot(qk.astype(bf16), v)
    cross = _dot(q, state.astype(bf16)) * q_decay
    kd = (ks * k_decay).astype(bf16)
    upd = lax.dot_general(kd, v, (((0,), (0,)), ((), ())), preferred_element_type=f32)
    state_ref[...] = state * chunk_decay + upd

    y = inner + cross
    mu = jnp.mean(y, axis=-1, keepdims=True)
    yc = y - mu
    var = jnp.mean(yc * yc, axis=-1, keepdims=True)
    yn = yc * lax.rsqrt(var + GN_EPS) * gain_ref[...]
    gt = g_ref[...].astype(f32)
    o_ref[...] = (gt * jax.nn.sigmoid(gt) * yn).astype(o_ref.dtype)


def retention_core(proj, gn_gain, batch, seq):
    nt, d4 = proj.shape
    d = d4 // 4
    heads = d // RET_HEAD_DIM
    nc = seq // RET_CHUNK
    hidx = jnp.arange(heads, dtype=f32)
    log_g = jnp.log1p(-jnp.exp2(-5.0 - hidx))
    lg = jnp.broadcast_to(log_g[:, None, None], (heads, 1, RET_HEAD_DIM))
    blk = (RET_CHUNK, RET_HEAD_DIM)

    def spec(off):
        return pl.BlockSpec(blk, lambda b, h, c: (b * nc + c, off * heads + h))

    return pl.pallas_call(
        _retention_kernel,
        out_shape=jax.ShapeDtypeStruct((nt, d), bf16),
        grid=(batch, heads, nc),
        in_specs=[pl.BlockSpec((None, 1, RET_HEAD_DIM), lambda b, h, c: (h, 0, 0)),
                  spec(0), spec(1), spec(2), spec(3),
                  pl.BlockSpec((1, RET_HEAD_DIM), lambda b, h, c: (0, h))],
        out_specs=pl.BlockSpec(blk, lambda b, h, c: (b * nc + c, h)),
        scratch_shapes=[pltpu.VMEM((RET_HEAD_DIM, RET_HEAD_DIM), f32)],
        compiler_params=_params(("parallel", "parallel", "arbitrary")),
        name="retention",
    )(lg, proj, proj, proj, proj, gn_gain.reshape(1, d))


def _ffn_in_kernel(h_ref, halo_ref, wa_ref, wu_ref, cw_ref, cb_ref, o_ref, *, tm, seq):
    i = pl.program_id(0)
    h = h_ref[...]
    wa = wa_ref[...]
    a = _dot(h, wa)
    u = _dot(h, wu_ref[...])
    ah = _dot(halo_ref[...], wa)
    seq_start = (i * tm) % seq == 0
    ah = jnp.where(seq_start, 0.0, ah)
    p1 = ah[BF16_SUBLANES - 1:BF16_SUBLANES, :]
    p2 = ah[BF16_SUBLANES - 2:BF16_SUBLANES - 1, :]
    row = lax.broadcasted_iota(jnp.int32, a.shape, 0)
    a1 = jnp.where(row == 0, p1, pltpu.roll(a, 1, 0))
    a2 = jnp.where(row == 0, p2, jnp.where(row == 1, p1, pltpu.roll(a, 2, 0)))
    cw = cw_ref[...]
    conv = a2 * cw[0:1, :] + a1 * cw[1:2, :] + a * cw[2:3, :] + cb_ref[...]
    o_ref[...] = (conv * jax.nn.sigmoid(conv) * u).astype(o_ref.dtype)


def ffn_in(h, w_in, conv_w, conv_b, seq):
    m, d = h.shape
    f = w_in.shape[1] // 2
    tm = _pick(seq, (1024, 512, 256, 128))
    tn = _pick(f, (256, 128))
    nf = f // tn
    hb = tm // BF16_SUBLANES
    return pl.pallas_call(
        functools.partial(_ffn_in_kernel, tm=tm, seq=seq),
        out_shape=jax.ShapeDtypeStruct((m, f), bf16),
        grid=(m // tm, nf),
        in_specs=[pl.BlockSpec((tm, d), lambda i, j: (i, 0)),
                  pl.BlockSpec((BF16_SUBLANES, d), lambda i, j: (jnp.maximum(i * hb - 1, 0), 0)),
                  pl.BlockSpec((d, tn), lambda i, j: (0, j)),
                  pl.BlockSpec((d, tn), lambda i, j: (0, nf + j)),
                  pl.BlockSpec((CONV_WIDTH, tn), lambda i, j: (0, j)),
                  pl.BlockSpec((1, tn), lambda i, j: (0, j))],
        out_specs=pl.BlockSpec((tm, tn), lambda i, j: (i, j)),
        compiler_params=_params(("parallel", "arbitrary")),
        name="ffn_in",
    )(h, h, w_in, w_in, conv_w, conv_b.reshape(1, f))


def _compress_kernel(t_ref, pos_ref, w1_ref, w2_ref, o_ref, acca_ref, accb_ref, *, n_cmp, n_r):
    r = pl.program_id(2)

    @pl.when(r == 0)
    def _():
        acca_ref[...] = jnp.zeros_like(acca_ref)
        accb_ref[...] = jnp.zeros_like(accb_ref)

    t = t_ref[...].astype(f32)
    xa = (t + pos_ref[0:1, :]).astype(bf16)
    xb = (t + pos_ref[1:2, :]).astype(bf16)
    acca_ref[...] += _dot(xa, w1_ref[0])
    accb_ref[...] += _dot(xb, w1_ref[1])

    @pl.when(r == n_r - 1)
    def _():
        rows = acca_ref.shape[0]
        hsum = acca_ref[...] + pltpu.roll(accb_ref[...], rows - 1, 0)
        rid = lax.broadcasted_iota(jnp.int32, hsum.shape, 0)
        hsum = jnp.where(rid < n_cmp, hsum, 0.0)
        act = hsum * jax.nn.sigmoid(hsum)
        o_ref[...] = _dot(act.astype(bf16), w2_ref[...]).astype(o_ref.dtype)


def compress_kv(kv, cmp_pos, cmp_w1, cmp_w2, batch, seq):
    g, dh = NSA_KV_GROUPS, NSA_HEAD_DIM
    ncols = kv.shape[1] // dh
    n_r = CMP_STRIDE
    slots = seq // CMP_STRIDE
    n_cmp = (seq - CMP_BLOCK) // CMP_STRIDE + 1
    halves = CMP_BLOCK // CMP_STRIDE
    assert halves == 2
    kv_v = kv.reshape(batch, slots, n_r * ncols * dh)
    pos = cmp_pos.reshape(2, halves, n_r, dh).transpose(0, 2, 1, 3)
    w1 = cmp_w1.reshape(2, halves, n_r, dh, dh).transpose(0, 2, 1, 3, 4).astype(bf16)
    w2 = cmp_w2.astype(bf16)
    return pl.pallas_call(
        functools.partial(_compress_kernel, n_cmp=n_cmp, n_r=n_r),
        out_shape=jax.ShapeDtypeStruct((batch, 2 * g, slots, dh), bf16),
        grid=(batch, 2 * g, n_r),
        in_specs=[pl.BlockSpec((None, slots, dh), lambda b, pg, r: (b, 0, r * ncols + pg)),
                  pl.BlockSpec((None, None, halves, dh), lambda b, pg, r: (pg // g, r, 0, 0)),
                  pl.BlockSpec((None, None, halves, dh, dh), lambda b, pg, r: (pg // g, r, 0, 0, 0)),
                  pl.BlockSpec((None, dh, dh), lambda b, pg, r: (pg // g, 0, 0))],
        out_specs=pl.BlockSpec((None, None, slots, dh), lambda b, pg, r: (b, pg, 0, 0)),
        scratch_shapes=[pltpu.VMEM((slots, dh), f32), pltpu.VMEM((slots, dh), f32)],
        compiler_params=_params(("parallel", "parallel", "arbitrary")),
        name="compress_kv",
    )(kv_v, pos, w1, w2)


def _nsa_kernel(slopes_ref, q_ref, gate_ref, kc_ref, vc_ref, ks_ref, vs_ref, kw_ref, vw_ref,
                o_ref, m_ref, l_ref, acc_ref, out_ref, *, tq, tk, hpg, n_cmp, n_blocks):
    g = pl.program_id(1)
    qi = pl.program_id(2)
    dh = NSA_HEAD_DIM
    scale = dh ** -0.5
    t0 = qi * tq
    t = t0 + lax.broadcasted_iota(jnp.int32, (tq, 1), 0)

    def head_slice(h):
        return slice(h * dh, (h + 1) * dh)

    def gate(h, c):
        return gate_ref[:, h * 3 + c:h * 3 + c + 1]

    ncp = kc_ref.shape[0]
    n_id = lax.broadcasted_iota(jnp.int32, (1, ncp), 1)
    dist_c = t - (n_id * CMP_STRIDE + (CMP_BLOCK - 1))
    mask_c = (dist_c >= 0) & (n_id < n_cmp)
    dist_cf = dist_c.astype(f32)
    kc = kc_ref[...]
    vc = vc_ref[...]
    psum = jnp.zeros((tq, ncp), f32)
    for h in range(hpg):
        slope = slopes_ref[g * hpg + h]
        s = _dot_nt(q_ref[:, head_slice(h)], kc) * scale - slope * dist_cf
        s = jnp.where(mask_c, s, NEG_BIG)
        mx = jnp.max(s, axis=-1, keepdims=True)
        e = jnp.where(mask_c, jnp.exp(s - mx), 0.0)
        p = e / jnp.maximum(jnp.sum(e, axis=-1, keepdims=True), 1e-30)
        psum = psum + p
        out_ref[:, head_slice(h)] = gate(h, 0) * _dot(p.astype(bf16), vc)

    n_i = lax.broadcasted_iota(jnp.int32, (ncp, LANES), 0)
    j_i = lax.broadcasted_iota(jnp.int32, (ncp, LANES), 1)
    ov = (jnp.minimum(n_i * CMP_STRIDE + CMP_BLOCK, (j_i + 1) * SEL_BLOCK)
          - jnp.maximum(n_i * CMP_STRIDE, j_i * SEL_BLOCK))
    ov = jnp.where((n_i < n_cmp) & (j_i < n_blocks), jnp.maximum(ov, 0), 0)
    sel_map = (ov.astype(f32) * (1.0 / CMP_STRIDE)).astype(bf16)
    p_hi = psum.astype(bf16)
    rem = psum - p_hi.astype(f32)
    p_mid = rem.astype(bf16)
    p_lo = (rem - p_mid.astype(f32)).astype(bf16)
    imp = _dot(p_lo, sel_map) + _dot(p_mid, sel_map) + _dot(p_hi, sel_map)

    j = lax.broadcasted_iota(jnp.int32, (tq, LANES), 1)
    cur = t >> SEL_SHIFT
    valid = (j * SEL_BLOCK <= t) & (j < n_blocks)
    forced = (j == 0) | (j == cur) | (j == cur - 1)
    sel = forced & valid
    score = jnp.where(valid & jnp.logical_not(forced), imp, -1.0)
    for _ in range(N_SELECT - 3):
        top = jnp.max(score, axis=-1, keepdims=True)
        hit = (score == top) & (score >= 0.0)
        sel = sel | hit
        score = jnp.where(hit, -1.0, score)
    sel_bf = jnp.where(sel, 1.0, 0.0).astype(bf16)

    def run_branch(k_ref, v_ref, lo, hi, allowed_fn, gate_col):
        m_ref[...] = jnp.full(m_ref.shape, NEG_BIG, f32)
        l_ref[...] = jnp.zeros(l_ref.shape, f32)
        acc_ref[...] = jnp.zeros(acc_ref.shape, f32)

        def body(kt, carry):
            k0 = pl.multiple_of(kt * tk, tk)
            kblk = k_ref[pl.ds(k0, tk), :]
            vblk = v_ref[pl.ds(k0, tk), :]
            key = k0 + lax.broadcasted_iota(jnp.int32, (1, tk), 1)
            dist = t - key
            allowed = allowed_fn(k0, dist)
            distf = dist.astype(f32)
            for h in range(hpg):
                slope = slopes_ref[g * hpg + h]
                s = _dot_nt(q_ref[:, head_slice(h)], kblk) * scale - slope * distf
                s = jnp.where(allowed, s, NEG_BIG)
                m_old = m_ref[h]
                m_new = jnp.maximum(m_old, jnp.max(s, axis=-1, keepdims=True))
                alpha = jnp.exp(m_old - m_new)
                p = jnp.where(allowed, jnp.exp(s - m_new), 0.0)
                l_ref[h] = alpha * l_ref[h] + jnp.sum(p, axis=-1, keepdims=True)
                acc_ref[h] = alpha * acc_ref[h] + _dot(p.astype(bf16), vblk)
                m_ref[h] = m_new
            return carry

        lax.fori_loop(lo, hi, body, 0)
        for h in range(hpg):
            o = acc_ref[h] / jnp.maximum(l_ref[h], 1e-30)
            out_ref[:, head_slice(h)] += gate(h, gate_col) * o

    last_tile = (t0 + tq - 1) // tk

    def allowed_sel(k0, dist):
        jb = lax.broadcasted_iota(jnp.int32, (LANES, tk), 0)
        kb = (k0 + lax.broadcasted_iota(jnp.int32, (LANES, tk), 1)) >> SEL_SHIFT
        expand = jnp.where(jb == kb, 1.0, 0.0).astype(bf16)
        return (_dot(sel_bf, expand) > 0.5) & (dist >= 0)

    run_branch(ks_ref, vs_ref, 0, last_tile + 1, allowed_sel, 1)

    def allowed_win(k0, dist):
        return (dist >= 0) & (dist < WINDOW)

    first_tile = jnp.maximum(t0 - (WINDOW - 1), 0) // tk
    run_branch(kw_ref, vw_ref, first_tile, last_tile + 1, allowed_win, 2)

    o_ref[...] = out_ref[...].astype(o_ref.dtype)


def nsa_attention(q, gate, kvc, kv, batch, seq):
    nt, d = q.shape
    g, dh = NSA_KV_GROUPS, NSA_HEAD_DIM
    heads = d // dh
    hpg = heads // g
    tq = _pick(seq, (256, 128))
    tk = tq
    nq = seq // tq
    n_cmp = (seq - CMP_BLOCK) // CMP_STRIDE + 1
    n_blocks = seq // SEL_BLOCK
    assert n_blocks <= LANES and n_blocks > 3
    slots = kvc.shape[2]
    slopes = jnp.exp2(-8.0 * jnp.arange(1, heads + 1, dtype=f32) / heads)
    kv3 = kv.reshape(batch, seq, kv.shape[1])

    def kv_spec(part):
        return pl.BlockSpec((None, seq, dh), lambda b, gg, qi: (b, 0, part * g + gg))

    return pl.pallas_call(
        functools.partial(_nsa_kernel, tq=tq, tk=tk, hpg=hpg, n_cmp=n_cmp, n_blocks=n_blocks),
        out_shape=jax.ShapeDtypeStruct((nt, d), bf16),
        grid=(batch, g, nq),
        in_specs=[pl.BlockSpec(memory_space=pltpu.SMEM),
                  pl.BlockSpec((tq, hpg * dh), lambda b, gg, qi: (b * nq + qi, gg)),
                  pl.BlockSpec((tq, LANES), lambda b, gg, qi: (b * nq + qi, gg)),
                  pl.BlockSpec((None, None, slots, dh), lambda b, gg, qi: (b, gg, 0, 0)),
                  pl.BlockSpec((None, None, slots, dh), lambda b, gg, qi: (b, g + gg, 0, 0)),
                  kv_spec(2), kv_spec(3), kv_spec(4), kv_spec(5)],
        out_specs=pl.BlockSpec((tq, hpg * dh), lambda b, gg, qi: (b * nq + qi, gg)),
        scratch_shapes=[pltpu.VMEM((hpg, tq, 1), f32), pltpu.VMEM((hpg, tq, 1), f32),
                        pltpu.VMEM((hpg, tq, dh), f32), pltpu.VMEM((tq, hpg * dh), f32)],
        compiler_params=_params(("parallel", "parallel", "arbitrary")),
        name="nsa_attention",
    )(slopes, q, gate, kvc, kvc, kv3, kv3, kv3, kv3)


def _gate_weights(w_gate, heads):
    d = w_gate.shape[0]
    g = NSA_KV_GROUPS
    per = (heads // g) * 3
    wg = w_gate.reshape(d, g, per)
    wg = jnp.pad(wg, ((0, 0), (0, 0), (0, LANES - per)))
    return wg.reshape(d, g * LANES)


def kernel(x, attn_norm, ffn_norm, w_ret_in, ret_gn_gain, w_ret_out, kv_norm, w_kv, cmp_pos, cmp_w1,
           cmp_w2, w_nsa_q, w_nsa_out, w_ffn_in, conv_w, conv_b, w_ffn_out, final_norm):
    batch, seq, d = x.shape
    depth = attn_norm.shape[0]
    n_a = w_ret_in.shape[0]
    heads = d // NSA_HEAD_DIM
    xs = x.reshape(batch * seq, d)
    kvc = kv = None
    for layer in range(depth):
        if layer < n_a:
            (h,) = rmsnorm(xs, attn_norm[layer:layer + 1], bf16)
            proj = matmul(h, w_ret_in[layer].astype(bf16), name="ret_in")
            y = retention_core(proj, ret_gn_gain[layer], batch, seq)
            xs = matmul(y, w_ret_out[layer].astype(bf16), res=xs, out_dtype=f32, name="ret_out")
        else:
            b = layer - n_a
            if layer == n_a:
                h, hkv = rmsnorm(xs, jnp.stack([attn_norm[layer], kv_norm]), bf16)
                kv = matmul(hkv, w_kv.astype(bf16), name="kv_proj")
                kvc = compress_kv(kv, cmp_pos, cmp_w1, cmp_w2, batch, seq)
            else:
                (h,) = rmsnorm(xs, attn_norm[layer:layer + 1], bf16)
            wq = w_nsa_q[b]
            q = matmul(h, wq[:, :d].astype(bf16), name="nsa_q")
            gate = matmul(h, _gate_weights(wq[:, d:], heads).astype(bf16), out_dtype=f32, sigmoid=True,
                          name="nsa_gate")
            o = nsa_attention(q, gate, kvc, kv, batch, seq)
            xs = matmul(o, w_nsa_out[b].astype(bf16), res=xs, out_dtype=f32, name="nsa_out")
        (h,) = rmsnorm(xs, ffn_norm[layer:layer + 1], bf16)
        act = ffn_in(h, w_ffn_in[layer].astype(bf16), conv_w[layer], conv_b[layer], seq)
        xs = matmul(act, w_ffn_out[layer].astype(bf16), res=xs, out_dtype=f32, name="ffn_out")
    (out,) = rmsnorm(xs, final_norm.reshape(1, d), f32)
    return out.reshape(batch, seq, d)
```

```python
import functools
import math

import jax
import jax.numpy as jnp
from jax import lax
from jax.experimental import pallas as pl
from jax.experimental.pallas import tpu as pltpu

f32 = jnp.float32
bf16 = jnp.bfloat16

RET_HEAD_DIM = 256
RET_CHUNK = 128
NSA_HEAD_DIM = 128
NSA_KV_GROUPS = 4
CMP_BLOCK = 32
CMP_STRIDE = 16
SEL_BLOCK = 64
SEL_SHIFT = 6
N_SELECT = 16
WINDOW = 512
CONV_WIDTH = 3
NORM_EPS = 1e-6
GN_EPS = 1e-5
NEG_BIG = -1e30
M_INIT = -1e29
LOG2E = math.log2(math.e)

V7X_VMEM_BYTES = 64 * 1024 * 1024
VMEM_LIMIT = V7X_VMEM_BYTES - 8 * 1024 * 1024
LANES = 128
BF16_SUBLANES = 16


def _pick(n, candidates):
    for c in candidates:
        if c <= n and n % c == 0:
            return c
    return n


def _params(sem):
    return pltpu.CompilerParams(dimension_semantics=sem, vmem_limit_bytes=VMEM_LIMIT)


def _dot(a, b):
    return jnp.dot(a, b, preferred_element_type=f32)


def _dot_nt(a, b):
    return lax.dot_general(a, b, (((1,), (1,)), ((), ())), preferred_element_type=f32)


def _rmsnorm_kernel(x_ref, g_ref, *o_refs):
    x = x_ref[...]
    y = x * lax.rsqrt(jnp.mean(x * x, axis=-1, keepdims=True) + NORM_EPS)
    for n, o_ref in enumerate(o_refs):
        o_ref[...] = (y * g_ref[n:n + 1, :]).astype(o_ref.dtype)


def rmsnorm(x, gains, out_dtype):
    m, d = x.shape
    n = gains.shape[0]
    tm = _pick(m, (256, 128, 64, 32, 16, 8))
    outs = pl.pallas_call(
        _rmsnorm_kernel,
        out_shape=[jax.ShapeDtypeStruct((m, d), out_dtype)] * n,
        grid=(m // tm,),
        in_specs=[pl.BlockSpec((tm, d), lambda i: (i, 0)),
                  pl.BlockSpec((n, d), lambda i: (0, 0))],
        out_specs=[pl.BlockSpec((tm, d), lambda i: (i, 0))] * n,
        compiler_params=_params(("parallel",)),
        name="rmsnorm",
    )(x, gains)
    return outs


def _matmul_kernel(a_ref, w_ref, *rest, nk, has_res, sigmoid):
    if has_res:
        res_ref, rest = rest[0], rest[1:]
    o_ref = rest[0]

    def finish(r):
        if has_res:
            r = r + res_ref[...]
        if sigmoid:
            r = jax.nn.sigmoid(r)
        o_ref[...] = r.astype(o_ref.dtype)

    if nk == 1:
        finish(_dot(a_ref[...], w_ref[...]))
        return
    acc_ref = rest[1]
    k = pl.program_id(2)

    @pl.when(k == 0)
    def _():
        acc_ref[...] = jnp.zeros_like(acc_ref)

    acc_ref[...] += _dot(a_ref[...], w_ref[...])

    @pl.when(k == nk - 1)
    def _():
        finish(acc_ref[...])


def matmul(a, w, layer=0, res=None, out_dtype=bf16, sigmoid=False, tm=None, tn=None, tk=None, name="matmul"):
    m, kdim = a.shape
    n = w.shape[2]
    tm = tm or _pick(m, (1024, 512, 256, 128))
    tn = tn or _pick(n, (512, 256, 128))
    tk = tk or (kdim if kdim <= 4096 else _pick(kdim, (kdim // 2,)))
    nk = kdim // tk
    in_specs = [pl.BlockSpec((tm, tk), lambda i, j, k: (i, k)),
                pl.BlockSpec((None, tk, tn), lambda i, j, k: (layer, k, j))]
    args = [a, w]
    if res is not None:
        in_specs.append(pl.BlockSpec((tm, tn), lambda i, j, k: (i, j)))
        args.append(res)
    return pl.pallas_call(
        functools.partial(_matmul_kernel, nk=nk, has_res=res is not None, sigmoid=sigmoid),
        out_shape=jax.ShapeDtypeStruct((m, n), out_dtype),
        grid=(m // tm, n // tn, nk),
        in_specs=in_specs,
        out_specs=pl.BlockSpec((tm, tn), lambda i, j, k: (i, j)),
        scratch_shapes=[] if nk == 1 else [pltpu.VMEM((tm, tn), f32)],
        compiler_params=_params(("parallel", "parallel", "arbitrary")),
        name=name,
    )(*args)


def _matmul_w32_kernel(a_ref, w_ref, *rest, has_res, out_scale):
    if has_res:
        res_ref, rest = rest[0], rest[1:]
    o_ref, wbf_ref = rest

    @pl.when(pl.program_id(1) == 0)
    def _():
        wbf_ref[...] = w_ref[...].astype(bf16)

    r = _dot(a_ref[...], wbf_ref[...])
    if out_scale is not None:
        r = r * out_scale
    if has_res:
        r = r + res_ref[...]
    o_ref[...] = r.astype(o_ref.dtype)


def matmul_w32(a, w, layer, n_out, res=None, out_dtype=bf16, out_scale=None, name="matmul_w32"):
    m, kdim = a.shape
    tm = _pick(m, (1024, 512, 256, 128))
    tn = _pick(n_out, (512, 256, 128))
    in_specs = [pl.BlockSpec((tm, kdim), lambda j, i: (i, 0)),
                pl.BlockSpec((None, kdim, tn), lambda j, i: (layer, 0, j))]
    args = [a, w]
    if res is not None:
        in_specs.append(pl.BlockSpec((tm, tn), lambda j, i: (i, j)))
        args.append(res)
    return pl.pallas_call(
        functools.partial(_matmul_w32_kernel, has_res=res is not None, out_scale=out_scale),
        out_shape=jax.ShapeDtypeStruct((m, n_out), out_dtype),
        grid=(n_out // tn, m // tm),
        in_specs=in_specs,
        out_specs=pl.BlockSpec((tm, tn), lambda j, i: (i, j)),
        scratch_shapes=[pltpu.VMEM((kdim, tn), bf16)],
        compiler_params=_params(("parallel", "arbitrary")),
        name=name,
    )(*args)


def _retention_kernel(lg_ref, q_ref, k_ref, v_ref, g_ref, gain_ref, o_ref, state_ref):
    c = pl.program_id(2)
    C, dh = q_ref.shape

    @pl.when(c == 0)
    def _():
        state_ref[...] = jnp.zeros_like(state_ref)

    lg = lg_ref[...]
    lg_c = lg[:, :C]
    i_row = lax.broadcasted_iota(jnp.int32, (C, C), 0)
    i_col = lax.broadcasted_iota(jnp.int32, (C, C), 1)
    diff = (i_row - i_col).astype(f32)
    decay_mask = jnp.where(diff >= 0, jnp.exp(jnp.maximum(diff, 0.0) * lg_c), 0.0)
    pos = lax.broadcasted_iota(jnp.int32, (C, dh), 0).astype(f32)
    q_decay = jnp.exp((pos + 1.0) * lg)
    k_decay = jnp.exp((C - 1.0 - pos) * lg)
    chunk_decay = jnp.exp(C * lg)

    q = q_ref[...]
    ks = k_ref[...].astype(f32) * (dh ** -0.5)
    v = v_ref[...]
    state = state_ref[...]
    qk = _dot_nt(q, ks.astype(bf16)) * decay_mask
    inner = _dot(qk.astype(bf16), v)
    cross = _dot(q, state.astype(bf16)) * q_decay
    kd = (ks * k_decay).astype(bf16)
    upd = lax.dot_general(kd, v, (((0,), (0,)), ((), ())), preferred_element_type=f32)
    state_ref[...] = state * chunk_decay + upd

    y = inner + cross
    mu = jnp.mean(y, axis=-1, keepdims=True)
    yc = y - mu
    var = jnp.mean(yc * yc, axis=-1, keepdims=True)
    yn = yc * lax.rsqrt(var + GN_EPS) * gain_ref[...]
    gt = g_ref[...].astype(f32)
    o_ref[...] = (gt * jax.nn.sigmoid(gt) * yn).astype(o_ref.dtype)


def retention_core(proj, gn_gain, batch, seq):
    nt, d4 = proj.shape
    d = d4 // 4
    heads = d // RET_HEAD_DIM
    nc = seq // RET_CHUNK
    hidx = jnp.arange(heads, dtype=f32)
    log_g = jnp.log1p(-jnp.exp2(-5.0 - hidx))
    lg = jnp.broadcast_to(log_g[:, None, None], (heads, 1, RET_HEAD_DIM))
    blk = (RET_CHUNK, RET_HEAD_DIM)

    def spec(off):
        return pl.BlockSpec(blk, lambda b, h, c: (b * nc + c, off * heads + h))

    return pl.pallas_call(
        _retention_kernel,
        out_shape=jax.ShapeDtypeStruct((nt, d), bf16),
        grid=(batch, heads, nc),
        in_specs=[pl.BlockSpec((None, 1, RET_HEAD_DIM), lambda b, h, c: (h, 0, 0)),
                  spec(0), spec(1), spec(2), spec(3),
                  pl.BlockSpec((1, RET_HEAD_DIM), lambda b, h, c: (0, h))],
        out_specs=pl.BlockSpec(blk, lambda b, h, c: (b * nc + c, h)),
        scratch_shapes=[pltpu.VMEM((RET_HEAD_DIM, RET_HEAD_DIM), f32)],
        compiler_params=_params(("parallel", "parallel", "arbitrary")),
        name="retention",
    )(lg, proj, proj, proj, proj, gn_gain.reshape(1, d))


def _ffn_in_kernel(h_ref, halo_ref, wa_ref, wu_ref, cw_ref, cb_ref, o_ref, wab_ref, wub_ref, *, tm, seq):
    i = pl.program_id(1)

    @pl.when(i == 0)
    def _():
        wab_ref[...] = wa_ref[...].astype(bf16)
        wub_ref[...] = wu_ref[...].astype(bf16)

    h = h_ref[...]
    wa = wab_ref[...]
    a = _dot(h, wa)
    u = _dot(h, wub_ref[...])
    ah = _dot(halo_ref[...], wa)
    seq_start = (i * tm) % seq == 0
    ah = jnp.where(seq_start, 0.0, ah)
    p1 = ah[BF16_SUBLANES - 1:BF16_SUBLANES, :]
    p2 = ah[BF16_SUBLANES - 2:BF16_SUBLANES - 1, :]
    row = lax.broadcasted_iota(jnp.int32, a.shape, 0)
    a1 = jnp.where(row == 0, p1, pltpu.roll(a, 1, 0))
    a2 = jnp.where(row == 0, p2, jnp.where(row == 1, p1, pltpu.roll(a, 2, 0)))
    cw = cw_ref[...]
    conv = a2 * cw[0:1, :] + a1 * cw[1:2, :] + a * cw[2:3, :] + cb_ref[...]
    o_ref[...] = (conv * jax.nn.sigmoid(conv) * u).astype(o_ref.dtype)


def ffn_in(h, w_in, conv_w, conv_b, layer, seq):
    m, d = h.shape
    f = w_in.shape[2] // 2
    tm = _pick(seq, (1024, 512, 256, 128))
    tn = _pick(f, (256, 128))
    nf = f // tn
    hb = tm // BF16_SUBLANES
    return pl.pallas_call(
        functools.partial(_ffn_in_kernel, tm=tm, seq=seq),
        out_shape=jax.ShapeDtypeStruct((m, f), bf16),
        grid=(nf, m // tm),
        in_specs=[pl.BlockSpec((tm, d), lambda j, i: (i, 0)),
                  pl.BlockSpec((BF16_SUBLANES, d), lambda j, i: (jnp.maximum(i * hb - 1, 0), 0)),
                  pl.BlockSpec((None, d, tn), lambda j, i: (layer, 0, j)),
                  pl.BlockSpec((None, d, tn), lambda j, i: (layer, 0, nf + j)),
                  pl.BlockSpec((None, CONV_WIDTH, tn), lambda j, i: (layer, 0, j)),
                  pl.BlockSpec((None, 1, tn), lambda j, i: (layer, 0, j))],
        out_specs=pl.BlockSpec((tm, tn), lambda j, i: (i, j)),
        scratch_shapes=[pltpu.VMEM((d, tn), bf16), pltpu.VMEM((d, tn), bf16)],
        compiler_params=_params(("parallel", "arbitrary")),
        name="ffn_in",
    )(h, h, w_in, w_in, conv_w, conv_b.reshape(conv_b.shape[0], 1, f))


def _compress_kernel(t_ref, pos_ref, w1_ref, w2_ref, o_ref, acca_ref, accb_ref, *, n_cmp, n_r):
    r = pl.program_id(2)

    @pl.when(r == 0)
    def _():
        acca_ref[...] = jnp.zeros_like(acca_ref)
        accb_ref[...] = jnp.zeros_like(accb_ref)

    t = t_ref[...].astype(f32)
    xa = (t + pos_ref[0:1, :]).astype(bf16)
    xb = (t + pos_ref[1:2, :]).astype(bf16)
    acca_ref[...] += _dot(xa, w1_ref[0])
    accb_ref[...] += _dot(xb, w1_ref[1])

    @pl.when(r == n_r - 1)
    def _():
        rows = acca_ref.shape[0]
        hsum = acca_ref[...] + pltpu.roll(accb_ref[...], rows - 1, 0)
        rid = lax.broadcasted_iota(jnp.int32, hsum.shape, 0)
        hsum = jnp.where(rid < n_cmp, hsum, 0.0)
        act = hsum * jax.nn.sigmoid(hsum)
        o_ref[...] = _dot(act.astype(bf16), w2_ref[...]).astype(o_ref.dtype)


def compress_kv(kv, cmp_pos, cmp_w1, cmp_w2, batch, seq):
    g, dh = NSA_KV_GROUPS, NSA_HEAD_DIM
    ncols = kv.shape[1] // dh
    n_r = CMP_STRIDE
    slots = seq // CMP_STRIDE
    n_cmp = (seq - CMP_BLOCK) // CMP_STRIDE + 1
    halves = CMP_BLOCK // CMP_STRIDE
    assert halves == 2
    kv_v = kv.reshape(batch, slots, n_r * ncols * dh)
    pos = cmp_pos.reshape(2, halves, n_r, dh).transpose(0, 2, 1, 3)
    w1 = cmp_w1.reshape(2, halves, n_r, dh, dh).transpose(0, 2, 1, 3, 4).astype(bf16)
    w2 = cmp_w2.astype(bf16)
    return pl.pallas_call(
        functools.partial(_compress_kernel, n_cmp=n_cmp, n_r=n_r),
        out_shape=jax.ShapeDtypeStruct((batch, 2 * g, slots, dh), bf16),
        grid=(batch, 2 * g, n_r),
        in_specs=[pl.BlockSpec((None, slots, dh), lambda b, pg, r: (b, 0, r * ncols + pg)),
                  pl.BlockSpec((None, None, halves, dh), lambda b, pg, r: (pg // g, r, 0, 0)),
                  pl.BlockSpec((None, None, halves, dh, dh), lambda b, pg, r: (pg // g, r, 0, 0, 0)),
                  pl.BlockSpec((None, dh, dh), lambda b, pg, r: (pg // g, 0, 0))],
        out_specs=pl.BlockSpec((None, None, slots, dh), lambda b, pg, r: (b, pg, 0, 0)),
        scratch_shapes=[pltpu.VMEM((slots, dh), f32), pltpu.VMEM((slots, dh), f32)],
        compiler_params=_params(("parallel", "parallel", "arbitrary")),
        name="compress_kv",
    )(kv_v, pos, w1, w2)


def _nsa_kernel(slopes_ref, q_ref, gate_ref, kc_ref, vc_ref, ks_ref, vs_ref, kw_ref, vw_ref,
                o_ref, qaug_ref, s_ref, p_ref, m_ref, l_ref, alpha_ref, acc_ref, out_ref,
                *, tq, hpg, n_cmp, n_blocks):
    g = pl.program_id(1)
    qi = pl.program_id(2)
    dh = NSA_HEAD_DIM
    tk = tq
    t0 = qi * tq
    t = t0 + lax.broadcasted_iota(jnp.int32, (tq, 1), 0)

    def hrows(h):
        return slice(h * tq, (h + 1) * tq)

    def hcols(h):
        return slice(h * dh, (h + 1) * dh)

    def gate(h, c):
        return gate_ref[:, h * 3 + c:h * 3 + c + 1]

    def slope(h):
        return slopes_ref[g * hpg + h]

    def lanes2(x):
        return jnp.concatenate([x] * (tk // LANES), axis=1)

    for h in range(hpg):
        qaug_ref[hrows(h), 0:dh] = q_ref[:, hcols(h)]

    ncp = kc_ref.shape[0]
    n_id = lax.broadcasted_iota(jnp.int32, (1, ncp), 1)
    end_pos = n_id * CMP_STRIDE + (CMP_BLOCK - 1)
    mask_c = (t >= end_pos) & (n_id < n_cmp)
    end_rel = (end_pos - t0).astype(f32)
    s_ref[...] = _dot_nt(qaug_ref[:, 0:dh], kc_ref[...])
    psum = jnp.zeros((tq, ncp), f32)
    for h in range(hpg):
        sb = jnp.where(mask_c, s_ref[hrows(h), :] + slope(h) * end_rel, NEG_BIG)
        mx = jnp.max(sb, axis=-1, keepdims=True)
        e = jnp.where(mask_c, jnp.exp2(sb - mx), 0.0)
        p = e / jnp.maximum(jnp.sum(e, axis=-1, keepdims=True), 1e-30)
        psum = psum + p
        p_ref[hrows(h), :] = p.astype(bf16)
    oc = _dot(p_ref[...], vc_ref[...])
    for h in range(hpg):
        out_ref[:, hcols(h)] = gate(h, 0) * oc[hrows(h), :]

    n_i = lax.broadcasted_iota(jnp.int32, (ncp, LANES), 0)
    j_i = lax.broadcasted_iota(jnp.int32, (ncp, LANES), 1)
    ov = (jnp.minimum(n_i * CMP_STRIDE + CMP_BLOCK, (j_i + 1) * SEL_BLOCK)
          - jnp.maximum(n_i * CMP_STRIDE, j_i * SEL_BLOCK))
    ov = jnp.where((n_i < n_cmp) & (j_i < n_blocks), jnp.maximum(ov, 0), 0)
    sel_map = (ov.astype(f32) * (1.0 / CMP_STRIDE)).astype(bf16)
    p_hi = psum.astype(bf16)
    rem = psum - p_hi.astype(f32)
    p_mid = rem.astype(bf16)
    p_lo = (rem - p_mid.astype(f32)).astype(bf16)
    imp = _dot(p_lo, sel_map) + _dot(p_mid, sel_map) + _dot(p_hi, sel_map)

    j = lax.broadcasted_iota(jnp.int32, (tq, LANES), 1)
    cur = t >> SEL_SHIFT
    valid = (j * SEL_BLOCK <= t) & (j < n_blocks)
    forced = (j == 0) | (j == cur) | (j == cur - 1)
    sel = forced & valid
    score = jnp.where(valid & jnp.logical_not(forced), imp, -1.0)
    for _ in range(N_SELECT - 3):
        top = jnp.max(score, axis=-1, keepdims=True)
        hit = (score == top) & (score >= 0.0)
        sel = sel | hit
        score = jnp.where(hit, -1.0, score)
    not_sel = jnp.where(sel, 0.0, 1.0).astype(bf16)
    for h in range(hpg):
        qaug_ref[hrows(h), dh:2 * dh] = not_sel

    rr = lax.broadcasted_iota(jnp.int32, (tq, tk), 0)
    cc = lax.broadcasted_iota(jnp.int32, (tq, tk), 1)
    lane_k = lax.broadcasted_iota(jnp.int32, (1, tk), 1)

    def init_stats():
        m_ref[...] = jnp.full(m_ref.shape, M_INIT, f32)
        l_ref[...] = jnp.zeros(l_ref.shape, f32)
        acc_ref[...] = jnp.zeros(acc_ref.shape, f32)

    def tile_step(kt, scores, v_ref, mask):
        k0 = pl.multiple_of(kt * tk, tk)
        s_ref[...] = scores(k0)
        key_rel = (k0 - t0 + lane_k).astype(f32)
        for h in range(hpg):
            sb = s_ref[hrows(h), :] + slope(h) * key_rel
            if mask == "causal":
                sb = jnp.where(rr >= cc, sb, NEG_BIG)
            elif mask == "anti":
                sb = jnp.where(rr < cc, sb, NEG_BIG)
            s_ref[hrows(h), :] = sb
            m_old = m_ref[hrows(h), :]
            m_new = jnp.maximum(m_old, jnp.max(sb, axis=-1, keepdims=True))
            alpha_ref[hrows(h), :] = jnp.exp2(m_old - m_new)
            m_ref[hrows(h), :] = m_new
        for h in range(hpg):
            p = jnp.exp2(s_ref[hrows(h), :] - lanes2(m_ref[hrows(h), :]))
            l_ref[hrows(h), :] = (alpha_ref[hrows(h), :] * l_ref[hrows(h), :]
                                  + jnp.sum(p, axis=-1, keepdims=True))
            p_ref[hrows(h), :] = p.astype(bf16)
        acc_ref[...] = acc_ref[...] * alpha_ref[...] + _dot(p_ref[...], v_ref[pl.ds(k0, tk), :])

    def finish(c):
        for h in range(hpg):
            o = acc_ref[hrows(h), :] / jnp.maximum(l_ref[hrows(h), :], 1e-30)
            out_ref[:, hcols(h)] += gate(h, c) * o

    def scores_sel(k0):
        blk = (k0 + lax.broadcasted_iota(jnp.int32, (tk, LANES), 0)) >> SEL_SHIFT
        jj = lax.broadcasted_iota(jnp.int32, (tk, LANES), 1)
        ebias = jnp.where(blk == jj, NEG_BIG, 0.0).astype(bf16)
        kaug = jnp.concatenate([ks_ref[pl.ds(k0, tk), :], ebias], axis=1)
        return _dot_nt(qaug_ref[...], kaug)

    def scores_win(k0):
        return _dot_nt(qaug_ref[:, 0:dh], kw_ref[pl.ds(k0, tk), :])

    init_stats()

    def sel_body(kt, carry):
        tile_step(kt, scores_sel, vs_ref, None)
        return carry

    lax.fori_loop(0, qi, sel_body, 0)
    tile_step(qi, scores_sel, vs_ref, "causal")
    finish(1)

    init_stats()
    n_back = WINDOW // tk

    @pl.when(qi >= n_back)
    def _():
        tile_step(qi - n_back, scores_win, vw_ref, "anti")

    for back in range(n_back - 1, 0, -1):
        @pl.when(qi >= back)
        def _(back=back):
            tile_step(qi - back, scores_win, vw_ref, None)

    tile_step(qi, scores_win, vw_ref, "causal")
    finish(2)

    o_ref[...] = out_ref[...].astype(o_ref.dtype)


def nsa_attention(q, gate, kvc, kv, batch, seq):
    nt, d = q.shape
    g, dh = NSA_KV_GROUPS, NSA_HEAD_DIM
    heads = d // dh
    hpg = heads // g
    tq = 256
    nq = seq // tq
    n_cmp = (seq - CMP_BLOCK) // CMP_STRIDE + 1
    n_blocks = seq // SEL_BLOCK
    slots = kvc.shape[2]
    assert seq % tq == 0 and WINDOW % tq == 0 and slots == tq
    assert n_blocks <= LANES and n_blocks > 3
    slopes = jnp.exp2(-8.0 * jnp.arange(1, heads + 1, dtype=f32) / heads) * LOG2E
    kv3 = kv.reshape(batch, seq, kv.shape[1])
    rows = hpg * tq

    def kv_spec(part):
        return pl.BlockSpec((None, seq, dh), lambda b, gg, qi: (b, 0, part * g + gg))

    return pl.pallas_call(
        functools.partial(_nsa_kernel, tq=tq, hpg=hpg, n_cmp=n_cmp, n_blocks=n_blocks),
        out_shape=jax.ShapeDtypeStruct((nt, d), bf16),
        grid=(batch, g, nq),
        in_specs=[pl.BlockSpec(memory_space=pltpu.SMEM),
                  pl.BlockSpec((tq, hpg * dh), lambda b, gg, qi: (b * nq + qi, gg)),
                  pl.BlockSpec((tq, LANES), lambda b, gg, qi: (b * nq + qi, gg)),
                  pl.BlockSpec((None, None, slots, dh), lambda b, gg, qi: (b, gg, 0, 0)),
                  pl.BlockSpec((None, None, slots, dh), lambda b, gg, qi: (b, g + gg, 0, 0)),
                  kv_spec(2), kv_spec(3), kv_spec(4), kv_spec(5)],
        out_specs=pl.BlockSpec((tq, hpg * dh), lambda b, gg, qi: (b * nq + qi, gg)),
        scratch_shapes=[pltpu.VMEM((rows, 2 * dh), bf16),
                        pltpu.VMEM((rows, tq), f32),
                        pltpu.VMEM((rows, tq), bf16),
                        pltpu.VMEM((rows, LANES), f32),
                        pltpu.VMEM((rows, LANES), f32),
                        pltpu.VMEM((rows, LANES), f32),
                        pltpu.VMEM((rows, dh), f32),
                        pltpu.VMEM((tq, hpg * dh), f32)],
        compiler_params=_params(("parallel", "parallel", "arbitrary")),
        name="nsa_attention",
    )(slopes, q, gate, kvc, kvc, kv3, kv3, kv3, kv3)


def _gate_weights(w_gate, heads):
    d = w_gate.shape[0]
    g = NSA_KV_GROUPS
    per = (heads // g) * 3
    wg = w_gate.reshape(d, g, per)
    wg = jnp.pad(wg, ((0, 0), (0, 0), (0, LANES - per)))
    return wg.reshape(1, d, g * LANES).astype(bf16)


def kernel(x, attn_norm, ffn_norm, w_ret_in, ret_gn_gain, w_ret_out, kv_norm, w_kv, cmp_pos, cmp_w1,
           cmp_w2, w_nsa_q, w_nsa_out, w_ffn_in, conv_w, conv_b, w_ffn_out, final_norm):
    batch, seq, d = x.shape
    depth = attn_norm.shape[0]
    n_a = w_ret_in.shape[0]
    heads = d // NSA_HEAD_DIM
    q_scale = NSA_HEAD_DIM ** -0.5 * LOG2E
    xs = x.reshape(batch * seq, d)
    w_ffn_out_bf = w_ffn_out.astype(bf16)
    kvc = kv = None
    for layer in range(depth):
        if layer < n_a:
            (h,) = rmsnorm(xs, attn_norm[layer:layer + 1], bf16)
            proj = matmul_w32(h, w_ret_in, layer, 4 * d, name="ret_in")
            y = retention_core(proj, ret_gn_gain[layer], batch, seq)
            xs = matmul_w32(y, w_ret_out, layer, d, res=xs, out_dtype=f32, name="ret_out")
        else:
            b = layer - n_a
            if layer == n_a:
                h, hkv = rmsnorm(xs, jnp.stack([attn_norm[layer], kv_norm]), bf16)
                kv = matmul_w32(hkv, w_kv[None], 0, w_kv.shape[1], name="kv_proj")
                kvc = compress_kv(kv, cmp_pos, cmp_w1, cmp_w2, batch, seq)
            else:
                (h,) = rmsnorm(xs, attn_norm[layer:layer + 1], bf16)
            q = matmul_w32(h, w_nsa_q, b, d, out_scale=q_scale, name="nsa_q")
            gate = matmul(h, _gate_weights(w_nsa_q[b][:, d:], heads), out_dtype=f32, sigmoid=True,
                          name="nsa_gate")
            o = nsa_attention(q, gate, kvc, kv, batch, seq)
            xs = matmul_w32(o, w_nsa_out, b, d, res=xs, out_dtype=f32, name="nsa_out")
        (h,) = rmsnorm(xs, ffn_norm[layer:layer + 1], bf16)
        act = ffn_in(h, w_ffn_in, conv_w, conv_b, layer, seq)
        xs = matmul(act, w_ffn_out_bf, layer, res=xs, out_dtype=f32, name="ffn_out")
    (out,) = rmsnorm(xs, final_norm.reshape(1, d), f32)
    return out.reshape(batch, seq, d)
```

```python
import functools
import math

import jax
import jax.numpy as jnp
from jax import lax
from jax.experimental import pallas as pl
from jax.experimental.pallas import tpu as pltpu

f32 = jnp.float32
bf16 = jnp.bfloat16

RET_HEAD_DIM = 256
RET_CHUNK = 128
NSA_HEAD_DIM = 128
NSA_KV_GROUPS = 4
CMP_BLOCK = 32
CMP_STRIDE = 16
SEL_BLOCK = 64
SEL_SHIFT = 6
N_SELECT = 16
WINDOW = 512
CONV_WIDTH = 3
NORM_EPS = 1e-6
GN_EPS = 1e-5
NEG_BIG = -1e30
M_INIT = -1e29
LOG2E = math.log2(math.e)
LANES = 128
ALIBI_LANE = LANES - 8

V7X_VMEM_BYTES = 64 * 1024 * 1024
VMEM_LIMIT = V7X_VMEM_BYTES - 8 * 1024 * 1024
BF16_SUBLANES = 16


def _pick(n, candidates):
    for c in candidates:
        if c <= n and n % c == 0:
            return c
    return n


def _params(sem):
    return pltpu.CompilerParams(dimension_semantics=sem, vmem_limit_bytes=VMEM_LIMIT)


def _dot(a, b):
    return jnp.dot(a, b, preferred_element_type=f32)


def _dot_nt(a, b):
    return lax.dot_general(a, b, (((1,), (1,)), ((), ())), preferred_element_type=f32)


def _rmsnorm_kernel(x_ref, g_ref, *o_refs):
    x = x_ref[...]
    y = x * lax.rsqrt(jnp.mean(x * x, axis=-1, keepdims=True) + NORM_EPS)
    for n, o_ref in enumerate(o_refs):
        o_ref[...] = (y * g_ref[n:n + 1, :]).astype(o_ref.dtype)


def rmsnorm(x, gains, out_dtype):
    m, d = x.shape
    n = gains.shape[0]
    tm = _pick(m, (256, 128, 64, 32, 16, 8))
    outs = pl.pallas_call(
        _rmsnorm_kernel,
        out_shape=[jax.ShapeDtypeStruct((m, d), out_dtype)] * n,
        grid=(m // tm,),
        in_specs=[pl.BlockSpec((tm, d), lambda i: (i, 0)),
                  pl.BlockSpec((n, d), lambda i: (0, 0))],
        out_specs=[pl.BlockSpec((tm, d), lambda i: (i, 0))] * n,
        compiler_params=_params(("parallel",)),
        name="rmsnorm",
    )(x, gains)
    return outs


def _matmul_kernel(a_ref, w_ref, *rest, nk, has_res, sigmoid):
    if has_res:
        res_ref, rest = rest[0], rest[1:]
    o_ref = rest[0]

    def finish(r):
        if has_res:
            r = r + res_ref[...]
        if sigmoid:
            r = jax.nn.sigmoid(r)
        o_ref[...] = r.astype(o_ref.dtype)

    if nk == 1:
        finish(_dot(a_ref[...], w_ref[...]))
        return
    acc_ref = rest[1]
    k = pl.program_id(2)

    @pl.when(k == 0)
    def _():
        acc_ref[...] = jnp.zeros_like(acc_ref)

    acc_ref[...] += _dot(a_ref[...], w_ref[...])

    @pl.when(k == nk - 1)
    def _():
        finish(acc_ref[...])


def matmul(a, w, layer=0, res=None, out_dtype=bf16, sigmoid=False, tm=None, tn=None, tk=None, name="matmul"):
    m, kdim = a.shape
    n = w.shape[2]
    tm = tm or _pick(m, (1024, 512, 256, 128))
    tn = tn or _pick(n, (512, 256, 128))
    tk = tk or (kdim if kdim <= 4096 else _pick(kdim, (kdim // 2,)))
    nk = kdim // tk
    in_specs = [pl.BlockSpec((tm, tk), lambda i, j, k: (i, k)),
                pl.BlockSpec((None, tk, tn), lambda i, j, k: (layer, k, j))]
    args = [a, w]
    if res is not None:
        in_specs.append(pl.BlockSpec((tm, tn), lambda i, j, k: (i, j)))
        args.append(res)
    return pl.pallas_call(
        functools.partial(_matmul_kernel, nk=nk, has_res=res is not None, sigmoid=sigmoid),
        out_shape=jax.ShapeDtypeStruct((m, n), out_dtype),
        grid=(m // tm, n // tn, nk),
        in_specs=in_specs,
        out_specs=pl.BlockSpec((tm, tn), lambda i, j, k: (i, j)),
        scratch_shapes=[] if nk == 1 else [pltpu.VMEM((tm, tn), f32)],
        compiler_params=_params(("parallel", "parallel", "arbitrary")),
        name=name,
    )(*args)


def _matmul_w32_kernel(a_ref, w_ref, *rest, has_res, out_scale):
    if has_res:
        res_ref, rest = rest[0], rest[1:]
    o_ref, wbf_ref = rest

    @pl.when(pl.program_id(1) == 0)
    def _():
        wbf_ref[...] = w_ref[...].astype(bf16)

    r = _dot(a_ref[...], wbf_ref[...])
    if out_scale is not None:
        r = r * out_scale
    if has_res:
        r = r + res_ref[...]
    o_ref[...] = r.astype(o_ref.dtype)


def matmul_w32(a, w, layer, n_out, res=None, out_dtype=bf16, out_scale=None, name="matmul_w32"):
    m, kdim = a.shape
    tm = _pick(m, (1024, 512, 256, 128))
    tn = _pick(n_out, (512, 256, 128))
    in_specs = [pl.BlockSpec((tm, kdim), lambda j, i: (i, 0)),
                pl.BlockSpec((None, kdim, tn), lambda j, i: (layer, 0, j))]
    args = [a, w]
    if res is not None:
        in_specs.append(pl.BlockSpec((tm, tn), lambda j, i: (i, j)))
        args.append(res)
    return pl.pallas_call(
        functools.partial(_matmul_w32_kernel, has_res=res is not None, out_scale=out_scale),
        out_shape=jax.ShapeDtypeStruct((m, n_out), out_dtype),
        grid=(n_out // tn, m // tm),
        in_specs=in_specs,
        out_specs=pl.BlockSpec((tm, tn), lambda j, i: (i, j)),
        scratch_shapes=[pltpu.VMEM((kdim, tn), bf16)],
        compiler_params=_params(("parallel", "arbitrary")),
        name=name,
    )(*args)


def _retention_kernel(q_ref, k_ref, v_ref, g_ref, gain_ref, o_ref, state_ref, mask_ref, qd_ref, kd_ref,
                      *, heads):
    b = pl.program_id(0)
    c = pl.program_id(1)
    C, dh = RET_CHUNK, RET_HEAD_DIM
    log_g = [math.log1p(-(2.0 ** (-5 - h))) for h in range(heads)]

    @pl.when((b == 0) & (c == 0))
    def _():
        diff = (lax.broadcasted_iota(jnp.int32, (C, C), 0)
                - lax.broadcasted_iota(jnp.int32, (C, C), 1)).astype(f32)
        pos = lax.broadcasted_iota(jnp.int32, (C, LANES), 0).astype(f32)
        for h in range(heads):
            mask_ref[h] = jnp.where(diff >= 0, jnp.exp(jnp.maximum(diff, 0.0) * log_g[h]), 0.0)
            qd_ref[h] = jnp.exp((pos + 1.0) * log_g[h])
            kd_ref[h] = jnp.exp((C - 1.0 - pos) * log_g[h])

    @pl.when(c == 0)
    def _():
        state_ref[...] = jnp.zeros_like(state_ref)

    def lanes2(x):
        return jnp.concatenate([x] * (dh // LANES), axis=1)

    for h in range(heads):
        sl = slice(h * dh, (h + 1) * dh)
        q = q_ref[:, sl]
        ks = k_ref[:, sl].astype(f32) * (dh ** -0.5)
        v = v_ref[:, sl]
        state = state_ref[h]
        qk = _dot_nt(q, ks.astype(bf16)) * mask_ref[h]
        inner = _dot(qk.astype(bf16), v)
        cross = _dot(q, state.astype(bf16)) * lanes2(qd_ref[h])
        kd = (ks * lanes2(kd_ref[h])).astype(bf16)
        upd = lax.dot_general(kd, v, (((0,), (0,)), ((), ())), preferred_element_type=f32)
        state_ref[h] = state * math.exp(C * log_g[h]) + upd

        y = inner + cross
        mu = jnp.mean(y, axis=-1, keepdims=True)
        yc = y - mu
        var = jnp.mean(yc * yc, axis=-1, keepdims=True)
        yn = yc * lax.rsqrt(var + GN_EPS) * gain_ref[:, sl]
        gt = g_ref[:, sl].astype(f32)
        o_ref[:, sl] = (gt * jax.nn.sigmoid(gt) * yn).astype(o_ref.dtype)


def retention_core(proj, gn_gain, batch, seq):
    nt, d4 = proj.shape
    d = d4 // 4
    heads = d // RET_HEAD_DIM
    nc = seq // RET_CHUNK
    blk = (RET_CHUNK, d)

    def spec(off):
        return pl.BlockSpec(blk, lambda b, c: (b * nc + c, off))

    return pl.pallas_call(
        functools.partial(_retention_kernel, heads=heads),
        out_shape=jax.ShapeDtypeStruct((nt, d), bf16),
        grid=(batch, nc),
        in_specs=[spec(0), spec(1), spec(2), spec(3),
                  pl.BlockSpec((1, d), lambda b, c: (0, 0))],
        out_specs=pl.BlockSpec(blk, lambda b, c: (b * nc + c, 0)),
        scratch_shapes=[pltpu.VMEM((heads, RET_HEAD_DIM, RET_HEAD_DIM), f32),
                        pltpu.VMEM((heads, RET_CHUNK, RET_CHUNK), f32),
                        pltpu.VMEM((heads, RET_CHUNK, LANES), f32),
                        pltpu.VMEM((heads, RET_CHUNK, LANES), f32)],
        compiler_params=_params(("arbitrary", "arbitrary")),
        name="retention",
    )(proj, proj, proj, proj, gn_gain.reshape(1, d))


def _ffn_in_kernel(h_ref, halo_ref, wa_ref, wu_ref, cw_ref, cb_ref, o_ref, wab_ref, wub_ref, *, tm, seq):
    i = pl.program_id(1)

    @pl.when(i == 0)
    def _():
        wab_ref[...] = wa_ref[...].astype(bf16)
        wub_ref[...] = wu_ref[...].astype(bf16)

    h = h_ref[...]
    wa = wab_ref[...]
    a = _dot(h, wa)
    u = _dot(h, wub_ref[...])
    ah = _dot(halo_ref[...], wa)
    seq_start = (i * tm) % seq == 0
    ah = jnp.where(seq_start, 0.0, ah)
    p1 = ah[BF16_SUBLANES - 1:BF16_SUBLANES, :]
    p2 = ah[BF16_SUBLANES - 2:BF16_SUBLANES - 1, :]
    a1 = pltpu.roll(a, 1, 0)
    a2 = pltpu.roll(a, 2, 0)
    row = lax.broadcasted_iota(jnp.int32, (8, a.shape[1]), 0)
    top1 = jnp.where(row == 0, p1, a1[0:8, :])
    top2 = jnp.where(row == 0, p2, jnp.where(row == 1, p1, a2[0:8, :]))
    a1 = jnp.concatenate([top1, a1[8:, :]], axis=0)
    a2 = jnp.concatenate([top2, a2[8:, :]], axis=0)
    cw = cw_ref[...]
    conv = a2 * cw[0:1, :] + a1 * cw[1:2, :] + a * cw[2:3, :] + cb_ref[...]
    o_ref[...] = (conv * jax.nn.sigmoid(conv) * u).astype(o_ref.dtype)


def ffn_in(h, w_in, conv_w, conv_b, layer, seq):
    m, d = h.shape
    f = w_in.shape[2] // 2
    tm = _pick(seq, (1024, 512, 256, 128))
    tn = _pick(f, (256, 128))
    nf = f // tn
    hb = tm // BF16_SUBLANES
    return pl.pallas_call(
        functools.partial(_ffn_in_kernel, tm=tm, seq=seq),
        out_shape=jax.ShapeDtypeStruct((m, f), bf16),
        grid=(nf, m // tm),
        in_specs=[pl.BlockSpec((tm, d), lambda j, i: (i, 0)),
                  pl.BlockSpec((BF16_SUBLANES, d), lambda j, i: (jnp.maximum(i * hb - 1, 0), 0)),
                  pl.BlockSpec((None, d, tn), lambda j, i: (layer, 0, j)),
                  pl.BlockSpec((None, d, tn), lambda j, i: (layer, 0, nf + j)),
                  pl.BlockSpec((None, CONV_WIDTH, tn), lambda j, i: (layer, 0, j)),
                  pl.BlockSpec((None, 1, tn), lambda j, i: (layer, 0, j))],
        out_specs=pl.BlockSpec((tm, tn), lambda j, i: (i, j)),
        scratch_shapes=[pltpu.VMEM((d, tn), bf16), pltpu.VMEM((d, tn), bf16)],
        compiler_params=_params(("parallel", "arbitrary")),
        name="ffn_in",
    )(h, h, w_in, w_in, conv_w, conv_b.reshape(conv_b.shape[0], 1, f))


def _compress_kernel(t_ref, pos_ref, w1_ref, w2_ref, o_ref, acca_ref, accb_ref, *, n_cmp, n_r):
    r = pl.program_id(2)

    @pl.when(r == 0)
    def _():
        acca_ref[...] = jnp.zeros_like(acca_ref)
        accb_ref[...] = jnp.zeros_like(accb_ref)

    t = t_ref[...].astype(f32)
    xa = (t + pos_ref[0:1, :]).astype(bf16)
    xb = (t + pos_ref[1:2, :]).astype(bf16)
    acca_ref[...] += _dot(xa, w1_ref[0])
    accb_ref[...] += _dot(xb, w1_ref[1])

    @pl.when(r == n_r - 1)
    def _():
        rows = acca_ref.shape[0]
        hsum = acca_ref[...] + pltpu.roll(accb_ref[...], rows - 1, 0)
        rid = lax.broadcasted_iota(jnp.int32, hsum.shape, 0)
        hsum = jnp.where(rid < n_cmp, hsum, 0.0)
        act = hsum * jax.nn.sigmoid(hsum)
        o_ref[...] = _dot(act.astype(bf16), w2_ref[...]).astype(o_ref.dtype)


def compress_kv(kv, cmp_pos, cmp_w1, cmp_w2, batch, seq):
    g, dh = NSA_KV_GROUPS, NSA_HEAD_DIM
    ncols = kv.shape[1] // dh
    n_r = CMP_STRIDE
    slots = seq // CMP_STRIDE
    n_cmp = (seq - CMP_BLOCK) // CMP_STRIDE + 1
    halves = CMP_BLOCK // CMP_STRIDE
    assert halves == 2
    kv_v = kv.reshape(batch, slots, n_r * ncols * dh)
    pos = cmp_pos.reshape(2, halves, n_r, dh).transpose(0, 2, 1, 3)
    w1 = cmp_w1.reshape(2, halves, n_r, dh, dh).transpose(0, 2, 1, 3, 4).astype(bf16)
    w2 = cmp_w2.astype(bf16)
    return pl.pallas_call(
        functools.partial(_compress_kernel, n_cmp=n_cmp, n_r=n_r),
        out_shape=jax.ShapeDtypeStruct((batch, 2 * g, slots, dh), bf16),
        grid=(batch, 2 * g, n_r),
        in_specs=[pl.BlockSpec((None, slots, dh), lambda b, pg, r: (b, 0, r * ncols + pg)),
                  pl.BlockSpec((None, None, halves, dh), lambda b, pg, r: (pg // g, r, 0, 0)),
                  pl.BlockSpec((None, None, halves, dh, dh), lambda b, pg, r: (pg // g, r, 0, 0, 0)),
                  pl.BlockSpec((None, dh, dh), lambda b, pg, r: (pg // g, 0, 0))],
        out_specs=pl.BlockSpec((None, None, slots, dh), lambda b, pg, r: (b, pg, 0, 0)),
        scratch_shapes=[pltpu.VMEM((slots, dh), f32), pltpu.VMEM((slots, dh), f32)],
        compiler_params=_params(("parallel", "parallel", "arbitrary")),
        name="compress_kv",
    )(kv_v, pos, w1, w2)


def _nsa_kernel(slopes_ref, q_ref, gate_ref, kc_ref, vc_ref, ks_ref, vs_ref, kw_ref, vw_ref,
                o_ref, qaug_ref, s_ref, p_ref, m_ref, l_ref, alpha_ref, acc_ref, out_ref,
                *, tq, hpg, n_cmp, n_blocks):
    g = pl.program_id(1)
    qi = pl.program_id(2)
    dh = NSA_HEAD_DIM
    tk = tq
    t0 = qi * tq
    t = t0 + lax.broadcasted_iota(jnp.int32, (tq, 1), 0)

    def hrows(h):
        return slice(h * tq, (h + 1) * tq)

    def hcols(h):
        return slice(h * dh, (h + 1) * dh)

    def gate(h, c):
        return gate_ref[:, h * 3 + c:h * 3 + c + 1]

    def slope(h):
        return slopes_ref[g * hpg + h]

    def lanes2(x):
        return jnp.concatenate([x] * (tk // LANES), axis=1)

    jq = lax.broadcasted_iota(jnp.int32, (tq, LANES), 1)
    slope_lanes = []
    for h in range(hpg):
        sv = jnp.full((tq, LANES), slope(h), f32)
        s_hi = sv.astype(bf16).astype(f32)
        s_mid = (sv - s_hi).astype(bf16).astype(f32)
        s_lo = sv - s_hi - s_mid
        sl = jnp.where((jq == ALIBI_LANE) | (jq == ALIBI_LANE + 3), s_hi,
                       jnp.where((jq == ALIBI_LANE + 1) | (jq == ALIBI_LANE + 4), s_mid, s_lo))
        sl = jnp.where((jq >= ALIBI_LANE) & (jq < ALIBI_LANE + 6), sl, 0.0)
        slope_lanes.append(sl)
        qaug_ref[hrows(h), 0:dh] = q_ref[:, hcols(h)]
        qaug_ref[hrows(h), dh:2 * dh] = sl.astype(bf16)

    def key_side(nrows, off_256, off_low, neg_blocks=None):
        jj = lax.broadcasted_iota(jnp.int32, (nrows, LANES), 1)
        x = jnp.where((jj >= ALIBI_LANE) & (jj < ALIBI_LANE + 3), off_256,
                      jnp.where((jj >= ALIBI_LANE + 3) & (jj < ALIBI_LANE + 6), off_low, 0.0))
        if neg_blocks is not None:
            x = jnp.where(neg_blocks == jj, NEG_BIG, x)
        return x.astype(bf16)

    ncp = kc_ref.shape[0]
    n_id = lax.broadcasted_iota(jnp.int32, (1, ncp), 1)
    end_pos = n_id * CMP_STRIDE + (CMP_BLOCK - 1)
    mask_c = (t >= end_pos) & (n_id < n_cmp)
    end_col = lax.broadcasted_iota(jnp.int32, (ncp, LANES), 0) * CMP_STRIDE + (CMP_BLOCK - 1)
    kc_side = key_side(ncp, (((end_col >> 8) << 8) - t0).astype(f32), (end_col & 255).astype(f32))
    s_ref[...] = _dot_nt(qaug_ref[...], jnp.concatenate([kc_ref[...], kc_side], axis=1))
    psum = jnp.zeros((tq, ncp), f32)
    for h in range(hpg):
        sb = jnp.where(mask_c, s_ref[hrows(h), :], NEG_BIG)
        mx = jnp.maximum(jnp.max(sb, axis=-1, keepdims=True), M_INIT)
        e = jnp.exp2(sb - mx)
        p = e * (1.0 / jnp.maximum(jnp.sum(e, axis=-1, keepdims=True), 1e-30))
        psum = psum + p
        p_ref[hrows(h), :] = p.astype(bf16)
    oc = _dot(p_ref[...], vc_ref[...])
    for h in range(hpg):
        out_ref[:, hcols(h)] = gate(h, 0) * oc[hrows(h), :]

    n_i = lax.broadcasted_iota(jnp.int32, (ncp, LANES), 0)
    j_i = lax.broadcasted_iota(jnp.int32, (ncp, LANES), 1)
    ov = (jnp.minimum(n_i * CMP_STRIDE + CMP_BLOCK, (j_i + 1) * SEL_BLOCK)
          - jnp.maximum(n_i * CMP_STRIDE, j_i * SEL_BLOCK))
    ov = jnp.where((n_i < n_cmp) & (j_i < n_blocks), jnp.maximum(ov, 0), 0)
    sel_map = (ov.astype(f32) * (1.0 / CMP_STRIDE)).astype(bf16)
    p_hi = psum.astype(bf16)
    rem = psum - p_hi.astype(f32)
    p_mid = rem.astype(bf16)
    p_lo = (rem - p_mid.astype(f32)).astype(bf16)
    imp = _dot(p_lo, sel_map) + _dot(p_mid, sel_map) + _dot(p_hi, sel_map)

    j = lax.broadcasted_iota(jnp.int32, (tq, LANES), 1)
    cur = t >> SEL_SHIFT
    valid = (j * SEL_BLOCK <= t) & (j < n_blocks)
    forced = (j == 0) | (j == cur) | (j == cur - 1)
    sel = forced & valid
    score = jnp.where(valid & jnp.logical_not(forced), imp, -1.0)
    for _ in range(N_SELECT - 3):
        top = jnp.max(score, axis=-1, keepdims=True)
        hit = (score == top) & (score >= 0.0)
        sel = sel | hit
        score = jnp.where(hit, -1.0, score)
    not_sel = jnp.where(sel, 0.0, 1.0)
    for h in range(hpg):
        qaug_ref[hrows(h), dh:2 * dh] = jnp.where(jq < n_blocks, not_sel, slope_lanes[h]).astype(bf16)

    rr = lax.broadcasted_iota(jnp.int32, (tq, tk), 0)
    cc = lax.broadcasted_iota(jnp.int32, (tq, tk), 1)
    ones_blk = jnp.ones((tk, LANES), bf16)

    def init_stats():
        m_ref[...] = jnp.full(m_ref.shape, M_INIT, f32)
        l_ref[...] = jnp.zeros(l_ref.shape, f32)
        acc_ref[...] = jnp.zeros(acc_ref.shape, f32)

    def tile_step(kt, k_ref, v_ref, block_mask, mask):
        k0 = pl.multiple_of(kt * tk, tk)
        krow = lax.broadcasted_iota(jnp.int32, (tk, LANES), 0)
        side = key_side(tk, (k0 - t0).astype(f32), krow.astype(f32),
                        ((k0 + krow) >> SEL_SHIFT) if block_mask else None)
        s_ref[...] = _dot_nt(qaug_ref[...], jnp.concatenate([k_ref[pl.ds(k0, tk), :], side], axis=1))
        for h in range(hpg):
            s = s_ref[hrows(h), :]
            if mask is not None:
                s = jnp.where((rr >= cc) if mask == "causal" else (rr < cc), s, NEG_BIG)
                s_ref[hrows(h), :] = s
            m_old = m_ref[hrows(h), :]
            m_new = jnp.maximum(m_old, jnp.max(s, axis=-1, keepdims=True))
            alpha_ref[hrows(h), :] = jnp.exp2(m_old - m_new)
            m_ref[hrows(h), :] = m_new
        for h in range(hpg):
            p_ref[hrows(h), :] = jnp.exp2(s_ref[hrows(h), :] - lanes2(m_ref[hrows(h), :])).astype(bf16)
        pv = _dot(p_ref[...], jnp.concatenate([v_ref[pl.ds(k0, tk), :], ones_blk], axis=1))
        alpha = alpha_ref[...]
        acc_ref[...] = acc_ref[...] * alpha + pv[:, 0:dh]
        l_ref[...] = l_ref[...] * alpha + pv[:, dh:2 * dh]

    def finish(c):
        for h in range(hpg):
            o = acc_ref[hrows(h), :] / jnp.maximum(l_ref[hrows(h), :], 1e-30)
            out_ref[:, hcols(h)] += gate(h, c) * o

    init_stats()

    def sel_body(kt, carry):
        tile_step(kt, ks_ref, vs_ref, True, None)
        return carry

    lax.fori_loop(0, qi, sel_body, 0)
    tile_step(qi, ks_ref, vs_ref, True, "causal")
    finish(1)

    init_stats()
    n_back = WINDOW // tk

    @pl.when(qi >= n_back)
    def _():
        tile_step(qi - n_back, kw_ref, vw_ref, False, "anti")

    for back in range(n_back - 1, 0, -1):
        @pl.when(qi >= back)
        def _(back=back):
            tile_step(qi - back, kw_ref, vw_ref, False, None)

    tile_step(qi, kw_ref, vw_ref, False, "causal")
    finish(2)

    o_ref[...] = out_ref[...].astype(o_ref.dtype)


def nsa_attention(q, gate, kvc, kv, batch, seq):
    nt, d = q.shape
    g, dh = NSA_KV_GROUPS, NSA_HEAD_DIM
    heads = d // dh
    hpg = heads // g
    tq = 256
    nq = seq // tq
    n_cmp = (seq - CMP_BLOCK) // CMP_STRIDE + 1
    n_blocks = seq // SEL_BLOCK
    slots = kvc.shape[2]
    assert seq % tq == 0 and WINDOW % tq == 0 and slots == tq
    assert 3 < n_blocks <= ALIBI_LANE and seq <= 256 * 256
    slopes = jnp.exp2(-8.0 * jnp.arange(1, heads + 1, dtype=f32) / heads) * LOG2E
    kv3 = kv.reshape(batch, seq, kv.shape[1])
    rows = hpg * tq

    def kv_spec(part):
        return pl.BlockSpec((None, seq, dh), lambda b, gg, qi: (b, 0, part * g + gg))

    return pl.pallas_call(
        functools.partial(_nsa_kernel, tq=tq, hpg=hpg, n_cmp=n_cmp, n_blocks=n_blocks),
        out_shape=jax.ShapeDtypeStruct((nt, d), bf16),
        grid=(batch, g, nq),
        in_specs=[pl.BlockSpec(memory_space=pltpu.SMEM),
                  pl.BlockSpec((tq, hpg * dh), lambda b, gg, qi: (b * nq + qi, gg)),
                  pl.BlockSpec((tq, LANES), lambda b, gg, qi: (b * nq + qi, gg)),
                  pl.BlockSpec((None, None, slots, dh), lambda b, gg, qi: (b, gg, 0, 0)),
                  pl.BlockSpec((None, None, slots, dh), lambda b, gg, qi: (b, g + gg, 0, 0)),
                  kv_spec(2), kv_spec(3), kv_spec(4), kv_spec(5)],
        out_specs=pl.BlockSpec((tq, hpg * dh), lambda b, gg, qi: (b * nq + qi, gg)),
        scratch_shapes=[pltpu.VMEM((rows, 2 * dh), bf16),
                        pltpu.VMEM((rows, tq), f32),
                        pltpu.VMEM((rows, tq), bf16),
                        pltpu.VMEM((rows, LANES), f32),
                        pltpu.VMEM((rows, LANES), f32),
                        pltpu.VMEM((rows, LANES), f32),
                        pltpu.VMEM((rows, dh), f32),
                        pltpu.VMEM((tq, hpg * dh), f32)],
        compiler_params=_params(("parallel", "parallel", "arbitrary")),
        name="nsa_attention",
    )(slopes, q, gate, kvc, kvc, kv3, kv3, kv3, kv3)


def _gate_weights(w_gate, heads):
    d = w_gate.shape[0]
    g = NSA_KV_GROUPS
    per = (heads // g) * 3
    wg = w_gate.reshape(d, g, per)
    wg = jnp.pad(wg, ((0, 0), (0, 0), (0, LANES - per)))
    return wg.reshape(1, d, g * LANES).astype(bf16)


def kernel(x, attn_norm, ffn_norm, w_ret_in, ret_gn_gain, w_ret_out, kv_norm, w_kv, cmp_pos, cmp_w1,
           cmp_w2, w_nsa_q, w_nsa_out, w_ffn_in, conv_w, conv_b, w_ffn_out, final_norm):
    batch, seq, d = x.shape
    depth = attn_norm.shape[0]
    n_a = w_ret_in.shape[0]
    heads = d // NSA_HEAD_DIM
    q_scale = NSA_HEAD_DIM ** -0.5 * LOG2E
    xs = x.reshape(batch * seq, d)
    w_ffn_out_bf = w_ffn_out.astype(bf16)
    kvc = kv = None
    for layer in range(depth):
        if layer < n_a:
            (h,) = rmsnorm(xs, attn_norm[layer:layer + 1], bf16)
            proj = matmul_w32(h, w_ret_in, layer, 4 * d, name="ret_in")
            y = retention_core(proj, ret_gn_gain[layer], batch, seq)
            xs = matmul_w32(y, w_ret_out, layer, d, res=xs, out_dtype=f32, name="ret_out")
        else:
            b = layer - n_a
            if layer == n_a:
                h, hkv = rmsnorm(xs, jnp.stack([attn_norm[layer], kv_norm]), bf16)
                kv = matmul_w32(hkv, w_kv[None], 0, w_kv.shape[1], name="kv_proj")
                kvc = compress_kv(kv, cmp_pos, cmp_w1, cmp_w2, batch, seq)
            else:
                (h,) = rmsnorm(xs, attn_norm[layer:layer + 1], bf16)
            q = matmul_w32(h, w_nsa_q, b, d, out_scale=q_scale, name="nsa_q")
            gate = matmul(h, _gate_weights(w_nsa_q[b][:, d:], heads), out_dtype=f32, sigmoid=True,
                          name="nsa_gate")
            o = nsa_attention(q, gate, kvc, kv, batch, seq)
            xs = matmul_w32(o, w_nsa_out, b, d, res=xs, out_dtype=f32, name="nsa_out")
        (h,) = rmsnorm(xs, ffn_norm[layer:layer + 1], bf16)
        act = ffn_in(h, w_ffn_in, conv_w, conv_b, layer, seq)
        xs = matmul(act, w_ffn_out_bf, layer, res=xs, out_dtype=f32, name="ffn_out")
    (out,) = rmsnorm(xs, final_norm.reshape(1, d), f32)
    return out.reshape(batch, seq, d)
```

```python
import functools
import math

import jax
import jax.numpy as jnp
from jax import lax
from jax.experimental import pallas as pl
from jax.experimental.pallas import tpu as pltpu

f32 = jnp.float32
bf16 = jnp.bfloat16

RET_HEAD_DIM = 256
RET_CHUNK = 128
NSA_HEAD_DIM = 128
NSA_KV_GROUPS = 4
CMP_BLOCK = 32
CMP_STRIDE = 16
SEL_BLOCK = 64
SEL_SHIFT = 6
N_SELECT = 16
WINDOW = 512
CONV_WIDTH = 3
NORM_EPS = 1e-6
GN_EPS = 1e-5
NEG_BIG = -1e30
M_INIT = -1e29
LOG2E = math.log2(math.e)
LANES = 128
ALIBI_LANE = LANES - 8

V7X_VMEM_BYTES = 64 * 1024 * 1024
VMEM_LIMIT = V7X_VMEM_BYTES - 8 * 1024 * 1024
BF16_SUBLANES = 16


def _pick(n, candidates):
    for c in candidates:
        if c <= n and n % c == 0:
            return c
    return n


def _params(sem):
    return pltpu.CompilerParams(dimension_semantics=sem, vmem_limit_bytes=VMEM_LIMIT)


def _dot(a, b):
    return jnp.dot(a, b, preferred_element_type=f32)


def _dot_nt(a, b):
    return lax.dot_general(a, b, (((1,), (1,)), ((), ())), preferred_element_type=f32)


def _rmsnorm_kernel(x_ref, g_ref, *o_refs):
    x = x_ref[...]
    y = x * lax.rsqrt(jnp.mean(x * x, axis=-1, keepdims=True) + NORM_EPS)
    for n, o_ref in enumerate(o_refs):
        o_ref[...] = (y * g_ref[n:n + 1, :]).astype(o_ref.dtype)


def rmsnorm(x, gains, out_dtype):
    m, d = x.shape
    n = gains.shape[0]
    tm = _pick(m, (256, 128, 64, 32, 16, 8))
    outs = pl.pallas_call(
        _rmsnorm_kernel,
        out_shape=[jax.ShapeDtypeStruct((m, d), out_dtype)] * n,
        grid=(m // tm,),
        in_specs=[pl.BlockSpec((tm, d), lambda i: (i, 0)),
                  pl.BlockSpec((n, d), lambda i: (0, 0))],
        out_specs=[pl.BlockSpec((tm, d), lambda i: (i, 0))] * n,
        compiler_params=_params(("parallel",)),
        name="rmsnorm",
    )(x, gains)
    return outs


def _row_scale(ssq_ref, d, width):
    rs = lax.rsqrt(ssq_ref[...] * (1.0 / d) + NORM_EPS)
    return jnp.concatenate([rs] * (width // LANES), axis=1)


def _matmul_kernel(*refs, nk, has_res, sigmoid, norm_d, n_gains):
    it = iter(refs)
    a_ref, w_ref = next(it), next(it)
    res_ref = next(it) if has_res else None
    ssq_in_ref = next(it) if norm_d else None
    gains_ref = next(it) if n_gains else None
    o_ref = next(it)
    xg_refs = [next(it) for _ in range(n_gains)]
    ssq_out_ref = next(it) if n_gains else None
    acc_ref = next(it) if nk > 1 else None
    j = pl.program_id(1)

    def finish(r):
        if norm_d:
            r = r * _row_scale(ssq_in_ref, norm_d, r.shape[1])
        if has_res:
            r = r + res_ref[...]
        if sigmoid:
            r = jax.nn.sigmoid(r)
        o_ref[...] = r.astype(o_ref.dtype)
        if n_gains:
            for n, xg_ref in enumerate(xg_refs):
                xg_ref[...] = (r * gains_ref[n:n + 1, :]).astype(bf16)
            part = jnp.sum(r * r, axis=-1, keepdims=True)

            @pl.when(j == 0)
            def _():
                ssq_out_ref[...] = jnp.broadcast_to(part, ssq_out_ref.shape)

            @pl.when(j > 0)
            def _():
                ssq_out_ref[...] += part

    if nk == 1:
        finish(_dot(a_ref[...], w_ref[...]))
        return
    k = pl.program_id(2)

    @pl.when(k == 0)
    def _():
        acc_ref[...] = jnp.zeros_like(acc_ref)

    acc_ref[...] += _dot(a_ref[...], w_ref[...])

    @pl.when(k == nk - 1)
    def _():
        finish(acc_ref[...])


def matmul(a, w, layer=0, res=None, out_dtype=bf16, sigmoid=False, ssq=None, next_gains=None, name="matmul"):
    m, kdim = a.shape
    n = w.shape[2]
    tm = _pick(m, (1024, 512, 256, 128))
    tn = _pick(n, (512, 256, 128))
    tk = kdim if kdim <= 4096 else _pick(kdim, (kdim // 2,))
    nk = kdim // tk
    n_gains = 0 if next_gains is None else next_gains.shape[0]
    in_specs = [pl.BlockSpec((tm, tk), lambda i, j, k: (i, k)),
                pl.BlockSpec((None, tk, tn), lambda i, j, k: (layer, k, j))]
    args = [a, w]
    if res is not None:
        in_specs.append(pl.BlockSpec((tm, tn), lambda i, j, k: (i, j)))
        args.append(res)
    if ssq is not None:
        in_specs.append(pl.BlockSpec((tm, LANES), lambda i, j, k: (i, 0)))
        args.append(ssq)
    out_shape = [jax.ShapeDtypeStruct((m, n), out_dtype)]
    out_specs = [pl.BlockSpec((tm, tn), lambda i, j, k: (i, j))]
    if n_gains:
        in_specs.append(pl.BlockSpec((n_gains, tn), lambda i, j, k: (0, j)))
        args.append(next_gains)
        out_shape += [jax.ShapeDtypeStruct((m, n), bf16)] * n_gains + [jax.ShapeDtypeStruct((m, LANES), f32)]
        out_specs += [pl.BlockSpec((tm, tn), lambda i, j, k: (i, j))] * n_gains
        out_specs += [pl.BlockSpec((tm, LANES), lambda i, j, k: (i, 0))]
    outs = pl.pallas_call(
        functools.partial(_matmul_kernel, nk=nk, has_res=res is not None, sigmoid=sigmoid,
                          norm_d=kdim if ssq is not None else 0, n_gains=n_gains),
        out_shape=out_shape,
        grid=(m // tm, n // tn, nk),
        in_specs=in_specs,
        out_specs=out_specs,
        scratch_shapes=[] if nk == 1 else [pltpu.VMEM((tm, tn), f32)],
        compiler_params=_params(("parallel", "arbitrary", "arbitrary")),
        name=name,
    )(*args)
    if n_gains:
        return outs[0], list(outs[1:1 + n_gains]), outs[-1]
    return outs[0]


def _matmul_w32_kernel(*refs, norm_d, out_scale):
    it = iter(refs)
    a_ref, w_ref = next(it), next(it)
    ssq_ref = next(it) if norm_d else None
    o_ref, wbf_ref = next(it), next(it)

    @pl.when(pl.program_id(1) == 0)
    def _():
        wbf_ref[...] = w_ref[...].astype(bf16)

    r = _dot(a_ref[...], wbf_ref[...])
    if norm_d:
        r = r * _row_scale(ssq_ref, norm_d, r.shape[1])
    if out_scale is not None:
        r = r * out_scale
    o_ref[...] = r.astype(o_ref.dtype)


def matmul_w32(a, w, layer, n_out, col0=0, ssq=None, out_dtype=bf16, out_scale=None, name="matmul_w32"):
    m, kdim = a.shape
    tm = _pick(m, (1024, 512, 256, 128))
    tn = _pick(n_out, (512, 256, 128))
    assert col0 % tn == 0
    j0 = col0 // tn
    in_specs = [pl.BlockSpec((tm, kdim), lambda j, i: (i, 0)),
                pl.BlockSpec((None, kdim, tn), lambda j, i: (layer, 0, j0 + j))]
    args = [a, w]
    if ssq is not None:
        in_specs.append(pl.BlockSpec((tm, LANES), lambda j, i: (i, 0)))
        args.append(ssq)
    return pl.pallas_call(
        functools.partial(_matmul_w32_kernel, norm_d=kdim if ssq is not None else 0, out_scale=out_scale),
        out_shape=jax.ShapeDtypeStruct((m, n_out), out_dtype),
        grid=(n_out // tn, m // tm),
        in_specs=in_specs,
        out_specs=pl.BlockSpec((tm, tn), lambda j, i: (i, j)),
        scratch_shapes=[pltpu.VMEM((kdim, tn), bf16)],
        compiler_params=_params(("parallel", "arbitrary")),
        name=name,
    )(*args)


def _retention_kernel(q_ref, k_ref, v_ref, g_ref, gain_ref, o_ref, state_ref, mask_ref, qd_ref, kd_ref,
                      *, heads):
    b = pl.program_id(0)
    c = pl.program_id(1)
    C, dh = RET_CHUNK, RET_HEAD_DIM
    log_g = [math.log1p(-(2.0 ** (-5 - h))) for h in range(heads)]

    @pl.when((b == 0) & (c == 0))
    def _():
        diff = (lax.broadcasted_iota(jnp.int32, (C, C), 0)
                - lax.broadcasted_iota(jnp.int32, (C, C), 1)).astype(f32)
        pos = lax.broadcasted_iota(jnp.int32, (C, LANES), 0).astype(f32)
        for h in range(heads):
            mask_ref[h] = jnp.where(diff >= 0, jnp.exp(jnp.maximum(diff, 0.0) * log_g[h]), 0.0)
            qd_ref[h] = jnp.exp((pos + 1.0) * log_g[h])
            kd_ref[h] = jnp.exp((C - 1.0 - pos) * log_g[h])

    @pl.when(c == 0)
    def _():
        state_ref[...] = jnp.zeros_like(state_ref)

    def lanes2(x):
        return jnp.concatenate([x] * (dh // LANES), axis=1)

    for h in range(heads):
        sl = slice(h * dh, (h + 1) * dh)
        q = q_ref[:, sl]
        ks = k_ref[:, sl].astype(f32) * (dh ** -0.5)
        v = v_ref[:, sl]
        state = state_ref[h]
        qk = _dot_nt(q, ks.astype(bf16)) * mask_ref[h]
        inner = _dot(qk.astype(bf16), v)
        cross = _dot(q, state.astype(bf16)) * lanes2(qd_ref[h])
        kd = (ks * lanes2(kd_ref[h])).astype(bf16)
        upd = lax.dot_general(kd, v, (((0,), (0,)), ((), ())), preferred_element_type=f32)
        state_ref[h] = state * math.exp(C * log_g[h]) + upd

        y = inner + cross
        mu = jnp.mean(y, axis=-1, keepdims=True)
        yc = y - mu
        var = jnp.mean(yc * yc, axis=-1, keepdims=True)
        yn = yc * lax.rsqrt(var + GN_EPS) * gain_ref[:, sl]
        gt = g_ref[:, sl].astype(f32)
        o_ref[:, sl] = (gt * jax.nn.sigmoid(gt) * yn).astype(o_ref.dtype)


def retention_core(proj, gn_gain, batch, seq):
    nt, d4 = proj.shape
    d = d4 // 4
    heads = d // RET_HEAD_DIM
    nc = seq // RET_CHUNK
    blk = (RET_CHUNK, d)

    def spec(off):
        return pl.BlockSpec(blk, lambda b, c: (b * nc + c, off))

    return pl.pallas_call(
        functools.partial(_retention_kernel, heads=heads),
        out_shape=jax.ShapeDtypeStruct((nt, d), bf16),
        grid=(batch, nc),
        in_specs=[spec(0), spec(1), spec(2), spec(3),
                  pl.BlockSpec((1, d), lambda b, c: (0, 0))],
        out_specs=pl.BlockSpec(blk, lambda b, c: (b * nc + c, 0)),
        scratch_shapes=[pltpu.VMEM((heads, RET_HEAD_DIM, RET_HEAD_DIM), f32),
                        pltpu.VMEM((heads, RET_CHUNK, RET_CHUNK), f32),
                        pltpu.VMEM((heads, RET_CHUNK, LANES), f32),
                        pltpu.VMEM((heads, RET_CHUNK, LANES), f32)],
        compiler_params=_params(("arbitrary", "arbitrary")),
        name="retention",
    )(proj, proj, proj, proj, gn_gain.reshape(1, d))


def _ffn_in_kernel(h_ref, halo_ref, ssq_ref, ssq_halo_ref, wa_ref, wu_ref, cw_ref, cb_ref, o_ref,
                   wab_ref, wub_ref, *, tm, seq):
    i = pl.program_id(1)

    @pl.when(i == 0)
    def _():
        wab_ref[...] = wa_ref[...].astype(bf16)
        wub_ref[...] = wu_ref[...].astype(bf16)

    h = h_ref[...]
    wa = wab_ref[...]
    d, tn = wa.shape
    rs = _row_scale(ssq_ref, d, tn)
    a = _dot(h, wa) * rs
    u = _dot(h, wub_ref[...]) * rs
    ah = _dot(halo_ref[...], wa) * _row_scale(ssq_halo_ref, d, tn)
    seq_start = (i * tm) % seq == 0
    ah = jnp.where(seq_start, 0.0, ah)
    p1 = ah[BF16_SUBLANES - 1:BF16_SUBLANES, :]
    p2 = ah[BF16_SUBLANES - 2:BF16_SUBLANES - 1, :]
    a1 = pltpu.roll(a, 1, 0)
    a2 = pltpu.roll(a, 2, 0)
    row = lax.broadcasted_iota(jnp.int32, (8, a.shape[1]), 0)
    top1 = jnp.where(row == 0, p1, a1[0:8, :])
    top2 = jnp.where(row == 0, p2, jnp.where(row == 1, p1, a2[0:8, :]))
    a1 = jnp.concatenate([top1, a1[8:, :]], axis=0)
    a2 = jnp.concatenate([top2, a2[8:, :]], axis=0)
    cw = cw_ref[...]
    conv = a2 * cw[0:1, :] + a1 * cw[1:2, :] + a * cw[2:3, :] + cb_ref[...]
    o_ref[...] = (conv * jax.nn.sigmoid(conv) * u).astype(o_ref.dtype)


def ffn_in(xg, ssq, w_in, conv_w, conv_b, layer, seq):
    m, d = xg.shape
    f = w_in.shape[2] // 2
    tm = _pick(seq, (1024, 512, 256, 128))
    tn = _pick(f, (256, 128))
    nf = f // tn
    hb = tm // BF16_SUBLANES

    def halo(j, i):
        return (jnp.maximum(i * hb - 1, 0), 0)

    return pl.pallas_call(
        functools.partial(_ffn_in_kernel, tm=tm, seq=seq),
        out_shape=jax.ShapeDtypeStruct((m, f), bf16),
        grid=(nf, m // tm),
        in_specs=[pl.BlockSpec((tm, d), lambda j, i: (i, 0)),
                  pl.BlockSpec((BF16_SUBLANES, d), halo),
                  pl.BlockSpec((tm, LANES), lambda j, i: (i, 0)),
                  pl.BlockSpec((BF16_SUBLANES, LANES), halo),
                  pl.BlockSpec((None, d, tn), lambda j, i: (layer, 0, j)),
                  pl.BlockSpec((None, d, tn), lambda j, i: (layer, 0, nf + j)),
                  pl.BlockSpec((None, CONV_WIDTH, tn), lambda j, i: (layer, 0, j)),
                  pl.BlockSpec((None, 1, tn), lambda j, i: (layer, 0, j))],
        out_specs=pl.BlockSpec((tm, tn), lambda j, i: (i, j)),
        scratch_shapes=[pltpu.VMEM((d, tn), bf16), pltpu.VMEM((d, tn), bf16)],
        compiler_params=_params(("parallel", "arbitrary")),
        name="ffn_in",
    )(xg, xg, ssq, ssq, w_in, w_in, conv_w, conv_b.reshape(conv_b.shape[0], 1, f))


def _compress_kernel(t_ref, pos_ref, w1cat_ref, w1_ref, w2_ref, o_ref, *, n_cmp, n_r, g, ncols):
    dh = NSA_HEAD_DIM
    slots = o_ref.shape[1]
    rid = lax.broadcasted_iota(jnp.int32, (slots, dh), 0)
    for part in range(2):
        pos8 = jnp.broadcast_to(pos_ref[part], (8, pos_ref.shape[2])).astype(bf16)
        pos_term = _dot(pos8, w1_ref[part])[0:1, :]
        for gg in range(g):
            pg = part * g + gg
            acc = jnp.zeros((slots, 2 * dh), f32)
            for r in range(n_r):
                col = (r * ncols + pg) * dh
                acc = acc + _dot(t_ref[:, col:col + dh], w1cat_ref[part, r])
            pre = acc[:, 0:dh] + pltpu.roll(acc[:, dh:2 * dh], slots - 1, 0) + pos_term
            pre = jnp.where(rid < n_cmp, pre, 0.0)
            act = pre * jax.nn.sigmoid(pre)
            o_ref[pg] = _dot(act.astype(bf16), w2_ref[part]).astype(o_ref.dtype)


def compress_kv(kv, cmp_pos, cmp_w1, cmp_w2, batch, seq):
    g, dh = NSA_KV_GROUPS, NSA_HEAD_DIM
    ncols = kv.shape[1] // dh
    n_r = CMP_STRIDE
    slots = seq // CMP_STRIDE
    n_cmp = (seq - CMP_BLOCK) // CMP_STRIDE + 1
    halves = CMP_BLOCK // CMP_STRIDE
    assert halves == 2 and ncols == 2 * g
    kv_v = kv.reshape(batch, slots, n_r * ncols * dh)
    pos = cmp_pos.reshape(2, 1, CMP_BLOCK * dh)
    w1 = cmp_w1.astype(bf16)
    w1cat = w1.reshape(2, halves, n_r, dh, dh).transpose(0, 2, 3, 1, 4).reshape(2, n_r, dh, halves * dh)
    w2 = cmp_w2.astype(bf16)

    def whole(x):
        return pl.BlockSpec(x.shape, lambda b: (0,) * x.ndim)

    return pl.pallas_call(
        functools.partial(_compress_kernel, n_cmp=n_cmp, n_r=n_r, g=g, ncols=ncols),
        out_shape=jax.ShapeDtypeStruct((batch, 2 * g, slots, dh), bf16),
        grid=(batch,),
        in_specs=[pl.BlockSpec((None, slots, n_r * ncols * dh), lambda b: (b, 0, 0)),
                  whole(pos), whole(w1cat), whole(w1), whole(w2)],
        out_specs=pl.BlockSpec((None, 2 * g, slots, dh), lambda b: (b, 0, 0, 0)),
        compiler_params=_params(("parallel",)),
        name="compress_kv",
    )(kv_v, pos, w1cat, w1, w2)


def _nsa_kernel(slopes_ref, q_ref, gate_ref, kc_ref, vc_ref, ks_ref, vs_ref, kw_ref, vw_ref,
                o_ref, qaug_ref, sa_ref, p_ref, m_ref, l_ref, alpha_ref, acc_ref, out_ref,
                *, tq, hpg, n_cmp, n_blocks):
    g = pl.program_id(1)
    qi = pl.program_id(2)
    dh = NSA_HEAD_DIM
    tk = tq
    t0 = qi * tq
    t = t0 + lax.broadcasted_iota(jnp.int32, (tq, 1), 0)

    def hrows(h):
        return slice(h * tq, (h + 1) * tq)

    def hcols(h):
        return slice(h * dh, (h + 1) * dh)

    def gate(h, c):
        return gate_ref[:, h * 3 + c:h * 3 + c + 1]

    def slope(h):
        return slopes_ref[g * hpg + h]

    jq = lax.broadcasted_iota(jnp.int32, (tq, LANES), 1)
    slope_lanes = []
    for h in range(hpg):
        sv = jnp.full((tq, LANES), slope(h), f32)
        s_hi = sv.astype(bf16).astype(f32)
        s_mid = (sv - s_hi).astype(bf16).astype(f32)
        s_lo = sv - s_hi - s_mid
        sl = jnp.where((jq == ALIBI_LANE) | (jq == ALIBI_LANE + 3), s_hi,
                       jnp.where((jq == ALIBI_LANE + 1) | (jq == ALIBI_LANE + 4), s_mid, s_lo))
        sl = jnp.where((jq >= ALIBI_LANE) & (jq < ALIBI_LANE + 6), sl, 0.0)
        slope_lanes.append(sl)
        qaug_ref[hrows(h), 0:dh] = q_ref[:, hcols(h)]
        qaug_ref[hrows(h), dh:2 * dh] = sl.astype(bf16)

    def key_side(nrows, off_256, off_low, neg_blocks=None):
        jj = lax.broadcasted_iota(jnp.int32, (nrows, LANES), 1)
        x = jnp.where((jj >= ALIBI_LANE) & (jj < ALIBI_LANE + 3), off_256,
                      jnp.where((jj >= ALIBI_LANE + 3) & (jj < ALIBI_LANE + 6), off_low, 0.0))
        if neg_blocks is not None:
            x = jnp.where(neg_blocks == jj, NEG_BIG, x)
        return x.astype(bf16)

    ncp = kc_ref.shape[0]
    n_id = lax.broadcasted_iota(jnp.int32, (1, ncp), 1)
    end_pos = n_id * CMP_STRIDE + (CMP_BLOCK - 1)
    mask_c = (t >= end_pos) & (n_id < n_cmp)
    end_col = lax.broadcasted_iota(jnp.int32, (ncp, LANES), 0) * CMP_STRIDE + (CMP_BLOCK - 1)
    kc_side = key_side(ncp, (((end_col >> 8) << 8) - t0).astype(f32), (end_col & 255).astype(f32))
    sa_ref[:, 0:ncp] = _dot_nt(qaug_ref[...], jnp.concatenate([kc_ref[...], kc_side], axis=1))
    psum = jnp.zeros((tq, ncp), f32)
    for h in range(hpg):
        sb = jnp.where(mask_c, sa_ref[hrows(h), 0:ncp], NEG_BIG)
        mx = jnp.maximum(jnp.max(sb, axis=-1, keepdims=True), M_INIT)
        e = jnp.exp2(sb - mx)
        p = e * (1.0 / jnp.maximum(jnp.sum(e, axis=-1, keepdims=True), 1e-30))
        psum = psum + p
        p_ref[hrows(h), 0:ncp] = p.astype(bf16)
    oc = _dot(p_ref[:, 0:ncp], vc_ref[...])
    for h in range(hpg):
        out_ref[:, hcols(h)] = gate(h, 0) * oc[hrows(h), :]

    n_i = lax.broadcasted_iota(jnp.int32, (ncp, LANES), 0)
    j_i = lax.broadcasted_iota(jnp.int32, (ncp, LANES), 1)
    ov = (jnp.minimum(n_i * CMP_STRIDE + CMP_BLOCK, (j_i + 1) * SEL_BLOCK)
          - jnp.maximum(n_i * CMP_STRIDE, j_i * SEL_BLOCK))
    ov = jnp.where((n_i < n_cmp) & (j_i < n_blocks), jnp.maximum(ov, 0), 0)
    sel_map = (ov.astype(f32) * (1.0 / CMP_STRIDE)).astype(bf16)
    p_hi = psum.astype(bf16)
    rem = psum - p_hi.astype(f32)
    p_mid = rem.astype(bf16)
    p_lo = (rem - p_mid.astype(f32)).astype(bf16)
    imp = _dot(p_lo, sel_map) + _dot(p_mid, sel_map) + _dot(p_hi, sel_map)

    j = lax.broadcasted_iota(jnp.int32, (tq, LANES), 1)
    cur = t >> SEL_SHIFT
    valid = (j * SEL_BLOCK <= t) & (j < n_blocks)
    forced = (j == 0) | (j == cur) | (j == cur - 1)
    sel = forced & valid
    score = jnp.where(valid & jnp.logical_not(forced), imp, -1.0)
    for _ in range(N_SELECT - 3):
        top = jnp.max(score, axis=-1, keepdims=True)
        hit = (score == top) & (score >= 0.0)
        sel = sel | hit
        score = jnp.where(hit, -1.0, score)
    not_sel = jnp.where(sel, 0.0, 1.0)
    for h in range(hpg):
        qaug_ref[hrows(h), dh:2 * dh] = jnp.where(jq < n_blocks, not_sel, slope_lanes[h]).astype(bf16)

    rr = lax.broadcasted_iota(jnp.int32, (tq, tk), 0)
    cc = lax.broadcasted_iota(jnp.int32, (tq, tk), 1)

    def init_stats():
        m_ref[...] = jnp.full(m_ref.shape, M_INIT, f32)
        l_ref[...] = jnp.zeros(l_ref.shape, f32)
        acc_ref[...] = jnp.zeros(acc_ref.shape, f32)

    def tile_step(k0, width, k_ref, v_ref, block_mask, mask):
        krow = lax.broadcasted_iota(jnp.int32, (width, LANES), 0)
        side = key_side(width, (k0 - t0 + ((krow >> 8) << 8)).astype(f32), (krow & 255).astype(f32),
                        ((k0 + krow) >> SEL_SHIFT) if block_mask else None)
        sa_ref[:, 0:width] = _dot_nt(qaug_ref[...],
                                     jnp.concatenate([k_ref[pl.ds(k0, width), :], side], axis=1))
        for h in range(hpg):
            s = sa_ref[hrows(h), 0:width]
            if mask is not None:
                s = jnp.where((rr >= cc) if mask == "causal" else (rr < cc), s, NEG_BIG)
                sa_ref[hrows(h), 0:width] = s
            m_old = m_ref[hrows(h), :]
            m_new = jnp.maximum(m_old, jnp.max(s, axis=-1, keepdims=True))
            alpha_ref[hrows(h), :] = jnp.exp2(m_old - m_new)
            m_ref[hrows(h), :] = m_new
        for h in range(hpg):
            m_rep = jnp.concatenate([m_ref[hrows(h), :]] * (width // LANES), axis=1)
            p_ref[hrows(h), 0:width] = jnp.exp2(sa_ref[hrows(h), 0:width] - m_rep).astype(bf16)
        ones_blk = jnp.ones((width, LANES), bf16)
        pv = _dot(p_ref[:, 0:width], jnp.concatenate([v_ref[pl.ds(k0, width), :], ones_blk], axis=1))
        alpha = alpha_ref[...]
        acc_ref[...] = acc_ref[...] * alpha + pv[:, 0:dh]
        l_ref[...] = l_ref[...] * alpha + pv[:, dh:2 * dh]

    def finish(c):
        for h in range(hpg):
            o = acc_ref[hrows(h), :] / jnp.maximum(l_ref[hrows(h), :], 1e-30)
            out_ref[:, hcols(h)] += gate(h, c) * o

    def tile_start(kt):
        return pl.multiple_of(kt * tk, tk)

    init_stats()

    def sel_body(kt, carry):
        tile_step(tile_start(kt), tk, ks_ref, vs_ref, True, None)
        return carry

    lax.fori_loop(0, qi, sel_body, 0)
    tile_step(tile_start(qi), tk, ks_ref, vs_ref, True, "causal")
    finish(1)

    init_stats()
    n_back = WINDOW // tk

    @pl.when(qi >= n_back)
    def _():
        tile_step(tile_start(qi - n_back), tk, kw_ref, vw_ref, False, "anti")

    for back in range(n_back - 1, 0, -1):
        @pl.when(qi >= back)
        def _(back=back):
            tile_step(tile_start(qi - back), tk, kw_ref, vw_ref, False, None)

    tile_step(tile_start(qi), tk, kw_ref, vw_ref, False, "causal")
    finish(2)

    o_ref[...] = out_ref[...].astype(o_ref.dtype)


def nsa_attention(q, gate, kvc, kv, batch, seq):
    nt, d = q.shape
    g, dh = NSA_KV_GROUPS, NSA_HEAD_DIM
    heads = d // dh
    hpg = heads // g
    tq = 256
    nq = seq // tq
    n_cmp = (seq - CMP_BLOCK) // CMP_STRIDE + 1
    n_blocks = seq // SEL_BLOCK
    slots = kvc.shape[2]
    assert seq % tq == 0 and WINDOW % tq == 0 and slots == tq
    assert 3 < n_blocks <= ALIBI_LANE and seq <= 256 * 256
    slopes = jnp.exp2(-8.0 * jnp.arange(1, heads + 1, dtype=f32) / heads) * LOG2E
    kv3 = kv.reshape(batch, seq, kv.shape[1])
    rows = hpg * tq

    def kv_spec(part):
        return pl.BlockSpec((None, seq, dh), lambda b, gg, qi: (b, 0, part * g + gg))

    return pl.pallas_call(
        functools.partial(_nsa_kernel, tq=tq, hpg=hpg, n_cmp=n_cmp, n_blocks=n_blocks),
        out_shape=jax.ShapeDtypeStruct((nt, d), bf16),
        grid=(batch, g, nq),
        in_specs=[pl.BlockSpec(memory_space=pltpu.SMEM),
                  pl.BlockSpec((tq, hpg * dh), lambda b, gg, qi: (b * nq + qi, gg)),
                  pl.BlockSpec((tq, LANES), lambda b, gg, qi: (b * nq + qi, gg)),
                  pl.BlockSpec((None, None, slots, dh), lambda b, gg, qi: (b, gg, 0, 0)),
                  pl.BlockSpec((None, None, slots, dh), lambda b, gg, qi: (b, g + gg, 0, 0)),
                  kv_spec(0), kv_spec(1), kv_spec(2), kv_spec(3)],
        out_specs=pl.BlockSpec((tq, hpg * dh), lambda b, gg, qi: (b * nq + qi, gg)),
        scratch_shapes=[pltpu.VMEM((rows, 2 * dh), bf16),
                        pltpu.VMEM((rows, tq), f32),
                        pltpu.VMEM((rows, tq), bf16),
                        pltpu.VMEM((rows, LANES), f32),
                        pltpu.VMEM((rows, LANES), f32),
                        pltpu.VMEM((rows, LANES), f32),
                        pltpu.VMEM((rows, dh), f32),
                        pltpu.VMEM((tq, hpg * dh), f32)],
        compiler_params=_params(("parallel", "parallel", "arbitrary")),
        name="nsa_attention",
    )(slopes, q, gate, kvc, kvc, kv3, kv3, kv3, kv3)


def _gate_weights(w_gate, heads):
    d = w_gate.shape[0]
    g = NSA_KV_GROUPS
    per = (heads // g) * 3
    wg = w_gate.reshape(d, g, per)
    wg = jnp.pad(wg, ((0, 0), (0, 0), (0, LANES - per)))
    return wg.reshape(1, d, g * LANES).astype(bf16)


def kernel(x, attn_norm, ffn_norm, w_ret_in, ret_gn_gain, w_ret_out, kv_norm, w_kv, cmp_pos, cmp_w1,
           cmp_w2, w_nsa_q, w_nsa_out, w_ffn_in, conv_w, conv_b, w_ffn_out, final_norm):
    batch, seq, d = x.shape
    depth = attn_norm.shape[0]
    n_a = w_ret_in.shape[0]
    heads = d // NSA_HEAD_DIM
    q_scale = NSA_HEAD_DIM ** -0.5 * LOG2E
    xs = x.reshape(batch * seq, d)
    w_ret_out_bf = w_ret_out.astype(bf16)
    w_nsa_out_bf = w_nsa_out.astype(bf16)
    w_ffn_out_bf = w_ffn_out.astype(bf16)

    def mixer_gains(layer):
        if layer == n_a:
            return jnp.stack([attn_norm[layer], kv_norm])
        return attn_norm[layer:layer + 1]

    normed, ssq = rmsnorm(xs, mixer_gains(0), bf16), None
    kvc = kv = None
    for layer in range(depth):
        ffn_gain = ffn_norm[layer:layer + 1]
        if layer < n_a:
            proj = matmul_w32(normed[0], w_ret_in, layer, 4 * d, ssq=ssq, name="ret_in")
            y = retention_core(proj, ret_gn_gain[layer], batch, seq)
            xs, normed, ssq = matmul(y, w_ret_out_bf, layer, res=xs, out_dtype=f32, next_gains=ffn_gain,
                                     name="ret_out")
        else:
            b = layer - n_a
            if layer == n_a:
                n_cmp_cols = 2 * NSA_KV_GROUPS * NSA_HEAD_DIM
                kv_cmp = matmul_w32(normed[1], w_kv[None], 0, n_cmp_cols, ssq=ssq, name="kv_cmp_proj")
                kv = matmul_w32(normed[1], w_kv[None], 0, w_kv.shape[1] - n_cmp_cols, col0=n_cmp_cols,
                                ssq=ssq, name="kv_proj")
                kvc = compress_kv(kv_cmp, cmp_pos, cmp_w1, cmp_w2, batch, seq)
            q = matmul_w32(normed[0], w_nsa_q, b, d, ssq=ssq, out_scale=q_scale, name="nsa_q")
            gate = matmul(normed[0], _gate_weights(w_nsa_q[b][:, d:], heads), out_dtype=f32, sigmoid=True,
                          ssq=ssq, name="nsa_gate")
            o = nsa_attention(q, gate, kvc, kv, batch, seq)
            xs, normed, ssq = matmul(o, w_nsa_out_bf, b, res=xs, out_dtype=f32, next_gains=ffn_gain,
                                     name="nsa_out")
        act = ffn_in(normed[0], ssq, w_ffn_in, conv_w, conv_b, layer, seq)
        if layer + 1 < depth:
            xs, normed, ssq = matmul(act, w_ffn_out_bf, layer, res=xs, out_dtype=f32,
                                     next_gains=mixer_gains(layer + 1), name="ffn_out")
        else:
            xs = matmul(act, w_ffn_out_bf, layer, res=xs, out_dtype=f32, name="ffn_out")
    (out,) = rmsnorm(xs, final_norm.reshape(1, d), f32)
    return out.reshape(batch, seq, d)
```

```python
import functools
import math

import jax
import jax.numpy as jnp
from jax import lax
from jax.experimental import pallas as pl
from jax.experimental.pallas import tpu as pltpu

f32 = jnp.float32
bf16 = jnp.bfloat16

RET_HEAD_DIM = 256
RET_CHUNK = 128
NSA_HEAD_DIM = 128
NSA_KV_GROUPS = 4
CMP_BLOCK = 32
CMP_STRIDE = 16
SEL_BLOCK = 64
SEL_SHIFT = 6
N_SELECT = 16
WINDOW = 512
CONV_WIDTH = 3
NORM_EPS = 1e-6
GN_EPS = 1e-5
NEG_BIG = -1e30
M_INIT = -1e29
LOG2E = math.log2(math.e)
LANES = 128
ALIBI_LANE = LANES - 8

V7X_VMEM_BYTES = 64 * 1024 * 1024
VMEM_LIMIT = V7X_VMEM_BYTES - 8 * 1024 * 1024
BF16_SUBLANES = 16


def _pick(n, candidates):
    for c in candidates:
        if c <= n and n % c == 0:
            return c
    return n


def _params(sem):
    return pltpu.CompilerParams(dimension_semantics=sem, vmem_limit_bytes=VMEM_LIMIT)


def _dot(a, b):
    return jnp.dot(a, b, preferred_element_type=f32)


def _dot_nt(a, b):
    return lax.dot_general(a, b, (((1,), (1,)), ((), ())), preferred_element_type=f32)


def _rmsnorm_kernel(x_ref, g_ref, *o_refs):
    x = x_ref[...]
    y = x * lax.rsqrt(jnp.mean(x * x, axis=-1, keepdims=True) + NORM_EPS)
    for n, o_ref in enumerate(o_refs):
        o_ref[...] = (y * g_ref[n:n + 1, :]).astype(o_ref.dtype)


def rmsnorm(x, gains, out_dtype):
    m, d = x.shape
    n = gains.shape[0]
    tm = _pick(m, (256, 128, 64, 32, 16, 8))
    outs = pl.pallas_call(
        _rmsnorm_kernel,
        out_shape=[jax.ShapeDtypeStruct((m, d), out_dtype)] * n,
        grid=(m // tm,),
        in_specs=[pl.BlockSpec((tm, d), lambda i: (i, 0)),
                  pl.BlockSpec((n, d), lambda i: (0, 0))],
        out_specs=[pl.BlockSpec((tm, d), lambda i: (i, 0))] * n,
        compiler_params=_params(("parallel",)),
        name="rmsnorm",
    )(x, gains)
    return outs


def _row_scale(ssq_ref, d, width):
    rs = lax.rsqrt(ssq_ref[...] * (1.0 / d) + NORM_EPS)
    return jnp.concatenate([rs] * (width // LANES), axis=1)


def _matmul_kernel(*refs, has_res, sigmoid, norm_d, n_gains):
    it = iter(refs)
    a_ref, w_ref = next(it), next(it)
    res_ref = next(it) if has_res else None
    ssq_in_ref = next(it) if norm_d else None
    gains_ref = next(it) if n_gains else None
    o_ref = next(it)
    xg_refs = [next(it) for _ in range(n_gains)]
    ssq_out_ref = next(it) if n_gains else None
    j = pl.program_id(1)

    r = _dot(a_ref[...], w_ref[...])
    if norm_d:
        r = r * _row_scale(ssq_in_ref, norm_d, r.shape[1])
    if has_res:
        r = r + res_ref[...]
    if sigmoid:
        r = jax.nn.sigmoid(r)
    o_ref[...] = r.astype(o_ref.dtype)
    if n_gains:
        for n, xg_ref in enumerate(xg_refs):
            xg_ref[...] = (r * gains_ref[n:n + 1, :]).astype(bf16)
        part = jnp.sum(r * r, axis=-1, keepdims=True)

        @pl.when(j == 0)
        def _():
            ssq_out_ref[...] = jnp.broadcast_to(part, ssq_out_ref.shape)

        @pl.when(j > 0)
        def _():
            ssq_out_ref[...] += part


def matmul(a, w, layer=0, res=None, out_dtype=bf16, sigmoid=False, ssq=None, next_gains=None, name="matmul"):
    m, kdim = a.shape
    n = w.shape[2]
    tm = _pick(m, (1024, 512, 256, 128) if kdim <= 4096 else (512, 256, 128))
    tn = _pick(n, (512, 256, 128))
    n_gains = 0 if next_gains is None else next_gains.shape[0]
    in_specs = [pl.BlockSpec((tm, kdim), lambda i, j: (i, 0)),
                pl.BlockSpec((None, kdim, tn), lambda i, j: (layer, 0, j))]
    args = [a, w]
    if res is not None:
        in_specs.append(pl.BlockSpec((tm, tn), lambda i, j: (i, j)))
        args.append(res)
    if ssq is not None:
        in_specs.append(pl.BlockSpec((tm, LANES), lambda i, j: (i, 0)))
        args.append(ssq)
    out_shape = [jax.ShapeDtypeStruct((m, n), out_dtype)]
    out_specs = [pl.BlockSpec((tm, tn), lambda i, j: (i, j))]
    if n_gains:
        in_specs.append(pl.BlockSpec((n_gains, tn), lambda i, j: (0, j)))
        args.append(next_gains)
        out_shape += [jax.ShapeDtypeStruct((m, n), bf16)] * n_gains + [jax.ShapeDtypeStruct((m, LANES), f32)]
        out_specs += [pl.BlockSpec((tm, tn), lambda i, j: (i, j))] * n_gains
        out_specs += [pl.BlockSpec((tm, LANES), lambda i, j: (i, 0))]
    outs = pl.pallas_call(
        functools.partial(_matmul_kernel, has_res=res is not None, sigmoid=sigmoid,
                          norm_d=kdim if ssq is not None else 0, n_gains=n_gains),
        out_shape=out_shape,
        grid=(m // tm, n // tn),
        in_specs=in_specs,
        out_specs=out_specs,
        compiler_params=_params(("parallel", "arbitrary")),
        name=name,
    )(*args)
    if n_gains:
        return outs[0], list(outs[1:1 + n_gains]), outs[-1]
    return outs[0]


def _matmul_w32_kernel(*refs, norm_d, out_scale):
    it = iter(refs)
    a_ref, w_ref = next(it), next(it)
    ssq_ref = next(it) if norm_d else None
    o_ref, wbf_ref = next(it), next(it)

    @pl.when(pl.program_id(1) == 0)
    def _():
        wbf_ref[...] = w_ref[...].astype(bf16)

    r = _dot(a_ref[...], wbf_ref[...])
    if norm_d:
        r = r * _row_scale(ssq_ref, norm_d, r.shape[1])
    if out_scale is not None:
        r = r * out_scale
    o_ref[...] = r.astype(o_ref.dtype)


def matmul_w32(a, w, layer, n_out, col0=0, ssq=None, out_dtype=bf16, out_scale=None, name="matmul_w32"):
    m, kdim = a.shape
    tm = _pick(m, (1024, 512, 256, 128))
    tn = _pick(n_out, (512, 256, 128))
    assert col0 % tn == 0
    j0 = col0 // tn
    in_specs = [pl.BlockSpec((tm, kdim), lambda j, i: (i, 0)),
                pl.BlockSpec((None, kdim, tn), lambda j, i: (layer, 0, j0 + j))]
    args = [a, w]
    if ssq is not None:
        in_specs.append(pl.BlockSpec((tm, LANES), lambda j, i: (i, 0)))
        args.append(ssq)
    return pl.pallas_call(
        functools.partial(_matmul_w32_kernel, norm_d=kdim if ssq is not None else 0, out_scale=out_scale),
        out_shape=jax.ShapeDtypeStruct((m, n_out), out_dtype),
        grid=(n_out // tn, m // tm),
        in_specs=in_specs,
        out_specs=pl.BlockSpec((tm, tn), lambda j, i: (i, j)),
        scratch_shapes=[pltpu.VMEM((kdim, tn), bf16)],
        compiler_params=_params(("parallel", "arbitrary")),
        name=name,
    )(*args)


def _retention_kernel(q_ref, k_ref, v_ref, g_ref, gain_ref, o_ref, state_ref, mask_ref, qd_ref, kd_ref,
                      *, heads):
    b = pl.program_id(0)
    c = pl.program_id(1)
    C, dh = RET_CHUNK, RET_HEAD_DIM
    log_g = [math.log1p(-(2.0 ** (-5 - h))) for h in range(heads)]

    @pl.when((b == 0) & (c == 0))
    def _():
        diff = (lax.broadcasted_iota(jnp.int32, (C, C), 0)
                - lax.broadcasted_iota(jnp.int32, (C, C), 1)).astype(f32)
        pos = lax.broadcasted_iota(jnp.int32, (C, LANES), 0).astype(f32)
        for h in range(heads):
            mask_ref[h] = jnp.where(diff >= 0, jnp.exp(jnp.maximum(diff, 0.0) * log_g[h]), 0.0)
            qd_ref[h] = jnp.exp((pos + 1.0) * log_g[h])
            kd_ref[h] = jnp.exp((C - 1.0 - pos) * log_g[h])

    @pl.when(c == 0)
    def _():
        state_ref[...] = jnp.zeros_like(state_ref)

    def lanes2(x):
        return jnp.concatenate([x] * (dh // LANES), axis=1)

    for h in range(heads):
        sl = slice(h * dh, (h + 1) * dh)
        q = q_ref[:, sl]
        ks = k_ref[:, sl].astype(f32) * (dh ** -0.5)
        v = v_ref[:, sl]
        state = state_ref[h]
        qk = _dot_nt(q, ks.astype(bf16)) * mask_ref[h]
        inner = _dot(qk.astype(bf16), v)
        cross = _dot(q, state.astype(bf16)) * lanes2(qd_ref[h])
        kd = (ks * lanes2(kd_ref[h])).astype(bf16)
        upd = lax.dot_general(kd, v, (((0,), (0,)), ((), ())), preferred_element_type=f32)
        state_ref[h] = state * math.exp(C * log_g[h]) + upd

        y = inner + cross
        mu = jnp.mean(y, axis=-1, keepdims=True)
        yc = y - mu
        var = jnp.mean(yc * yc, axis=-1, keepdims=True)
        yn = yc * lax.rsqrt(var + GN_EPS) * gain_ref[:, sl]
        gt = g_ref[:, sl].astype(f32)
        o_ref[:, sl] = (gt * jax.nn.sigmoid(gt) * yn).astype(o_ref.dtype)


def retention_core(proj, gn_gain, batch, seq):
    nt, d4 = proj.shape
    d = d4 // 4
    heads = d // RET_HEAD_DIM
    nc = seq // RET_CHUNK
    blk = (RET_CHUNK, d)

    def spec(off):
        return pl.BlockSpec(blk, lambda b, c: (b * nc + c, off))

    return pl.pallas_call(
        functools.partial(_retention_kernel, heads=heads),
        out_shape=jax.ShapeDtypeStruct((nt, d), bf16),
        grid=(batch, nc),
        in_specs=[spec(0), spec(1), spec(2), spec(3),
                  pl.BlockSpec((1, d), lambda b, c: (0, 0))],
        out_specs=pl.BlockSpec(blk, lambda b, c: (b * nc + c, 0)),
        scratch_shapes=[pltpu.VMEM((heads, RET_HEAD_DIM, RET_HEAD_DIM), f32),
                        pltpu.VMEM((heads, RET_CHUNK, RET_CHUNK), f32),
                        pltpu.VMEM((heads, RET_CHUNK, LANES), f32),
                        pltpu.VMEM((heads, RET_CHUNK, LANES), f32)],
        compiler_params=_params(("arbitrary", "arbitrary")),
        name="retention",
    )(proj, proj, proj, proj, gn_gain.reshape(1, d))


def _ffn_in_kernel(h_ref, halo_ref, ssq_ref, ssq_halo_ref, wa_ref, wu_ref, cw_ref, cb_ref, o_ref,
                   wab_ref, wub_ref, *, tm, seq):
    i = pl.program_id(1)

    @pl.when(i == 0)
    def _():
        wab_ref[...] = wa_ref[...].astype(bf16)
        wub_ref[...] = wu_ref[...].astype(bf16)

    h = h_ref[...]
    wa = wab_ref[...]
    d, tn = wa.shape
    rs = _row_scale(ssq_ref, d, tn)
    a = _dot(h, wa) * rs
    u = _dot(h, wub_ref[...]) * rs
    ah = _dot(halo_ref[...], wa) * _row_scale(ssq_halo_ref, d, tn)
    seq_start = (i * tm) % seq == 0
    ah = jnp.where(seq_start, 0.0, ah)
    p1 = ah[BF16_SUBLANES - 1:BF16_SUBLANES, :]
    p2 = ah[BF16_SUBLANES - 2:BF16_SUBLANES - 1, :]
    a1 = pltpu.roll(a, 1, 0)
    a2 = pltpu.roll(a, 2, 0)
    row = lax.broadcasted_iota(jnp.int32, (8, a.shape[1]), 0)
    top1 = jnp.where(row == 0, p1, a1[0:8, :])
    top2 = jnp.where(row == 0, p2, jnp.where(row == 1, p1, a2[0:8, :]))
    a1 = jnp.concatenate([top1, a1[8:, :]], axis=0)
    a2 = jnp.concatenate([top2, a2[8:, :]], axis=0)
    cw = cw_ref[...]
    conv = a2 * cw[0:1, :] + a1 * cw[1:2, :] + a * cw[2:3, :] + cb_ref[...]
    o_ref[...] = (conv * jax.nn.sigmoid(conv) * u).astype(o_ref.dtype)


def ffn_in(xg, ssq, w_in, conv_w, conv_b, layer, seq):
    m, d = xg.shape
    f = w_in.shape[2] // 2
    tm = _pick(seq, (1024, 512, 256, 128))
    tn = _pick(f, (256, 128))
    nf = f // tn
    hb = tm // BF16_SUBLANES

    def halo(j, i):
        return (jnp.maximum(i * hb - 1, 0), 0)

    return pl.pallas_call(
        functools.partial(_ffn_in_kernel, tm=tm, seq=seq),
        out_shape=jax.ShapeDtypeStruct((m, f), bf16),
        grid=(nf, m // tm),
        in_specs=[pl.BlockSpec((tm, d), lambda j, i: (i, 0)),
                  pl.BlockSpec((BF16_SUBLANES, d), halo),
                  pl.BlockSpec((tm, LANES), lambda j, i: (i, 0)),
                  pl.BlockSpec((BF16_SUBLANES, LANES), halo),
                  pl.BlockSpec((None, d, tn), lambda j, i: (layer, 0, j)),
                  pl.BlockSpec((None, d, tn), lambda j, i: (layer, 0, nf + j)),
                  pl.BlockSpec((None, CONV_WIDTH, tn), lambda j, i: (layer, 0, j)),
                  pl.BlockSpec((None, 1, tn), lambda j, i: (layer, 0, j))],
        out_specs=pl.BlockSpec((tm, tn), lambda j, i: (i, j)),
        scratch_shapes=[pltpu.VMEM((d, tn), bf16), pltpu.VMEM((d, tn), bf16)],
        compiler_params=_params(("parallel", "arbitrary")),
        name="ffn_in",
    )(xg, xg, ssq, ssq, w_in, w_in, conv_w, conv_b.reshape(conv_b.shape[0], 1, f))


def _compress_kernel(t_ref, pos_ref, w1cat_ref, w1_ref, w2_ref, o_ref, *, n_cmp, n_r, g, ncols):
    dh = NSA_HEAD_DIM
    slots = o_ref.shape[1]
    rid = lax.broadcasted_iota(jnp.int32, (slots, dh), 0)
    for part in range(2):
        pos8 = jnp.broadcast_to(pos_ref[part], (8, pos_ref.shape[2])).astype(bf16)
        pos_term = _dot(pos8, w1_ref[part])[0:1, :]
        for gg in range(g):
            pg = part * g + gg
            acc = jnp.zeros((slots, 2 * dh), f32)
            for r in range(n_r):
                col = (r * ncols + pg) * dh
                acc = acc + _dot(t_ref[:, col:col + dh], w1cat_ref[part, r])
            pre = acc[:, 0:dh] + pltpu.roll(acc[:, dh:2 * dh], slots - 1, 0) + pos_term
            pre = jnp.where(rid < n_cmp, pre, 0.0)
            act = pre * jax.nn.sigmoid(pre)
            o_ref[pg] = _dot(act.astype(bf16), w2_ref[part]).astype(o_ref.dtype)


def compress_kv(kv, cmp_pos, cmp_w1, cmp_w2, batch, seq):
    g, dh = NSA_KV_GROUPS, NSA_HEAD_DIM
    ncols = kv.shape[1] // dh
    n_r = CMP_STRIDE
    slots = seq // CMP_STRIDE
    n_cmp = (seq - CMP_BLOCK) // CMP_STRIDE + 1
    halves = CMP_BLOCK // CMP_STRIDE
    assert halves == 2 and ncols == 2 * g
    kv_v = kv.reshape(batch, slots, n_r * ncols * dh)
    pos = cmp_pos.reshape(2, 1, CMP_BLOCK * dh)
    w1 = cmp_w1.astype(bf16)
    w1cat = w1.reshape(2, halves, n_r, dh, dh).transpose(0, 2, 3, 1, 4).reshape(2, n_r, dh, halves * dh)
    w2 = cmp_w2.astype(bf16)

    def whole(x):
        return pl.BlockSpec(x.shape, lambda b: (0,) * x.ndim)

    return pl.pallas_call(
        functools.partial(_compress_kernel, n_cmp=n_cmp, n_r=n_r, g=g, ncols=ncols),
        out_shape=jax.ShapeDtypeStruct((batch, 2 * g, slots, dh), bf16),
        grid=(batch,),
        in_specs=[pl.BlockSpec((None, slots, n_r * ncols * dh), lambda b: (b, 0, 0)),
                  whole(pos), whole(w1cat), whole(w1), whole(w2)],
        out_specs=pl.BlockSpec((None, 2 * g, slots, dh), lambda b: (b, 0, 0, 0)),
        compiler_params=_params(("parallel",)),
        name="compress_kv",
    )(kv_v, pos, w1cat, w1, w2)


def _nsa_kernel(slopes_ref, q_ref, gate_ref, kc_ref, vc_ref, ks_ref, vs_ref, kw_ref, vw_ref,
                o_ref, qaug_ref, sa_ref, p_ref, m_ref, l_ref, alpha_ref, acc_ref, out_ref,
                *, tq, hpg, n_cmp, n_blocks):
    g = pl.program_id(1)
    qi = pl.program_id(2)
    dh = NSA_HEAD_DIM
    tk = tq
    t0 = qi * tq
    t = t0 + lax.broadcasted_iota(jnp.int32, (tq, 1), 0)

    def hrows(h):
        return slice(h * tq, (h + 1) * tq)

    def hcols(h):
        return slice(h * dh, (h + 1) * dh)

    def gate(h, c):
        return gate_ref[:, h * 3 + c:h * 3 + c + 1]

    def slope(h):
        return slopes_ref[g * hpg + h]

    jq = lax.broadcasted_iota(jnp.int32, (tq, LANES), 1)
    slope_lanes = []
    for h in range(hpg):
        sv = jnp.full((tq, LANES), slope(h), f32)
        s_hi = sv.astype(bf16).astype(f32)
        s_mid = (sv - s_hi).astype(bf16).astype(f32)
        s_lo = sv - s_hi - s_mid
        sl = jnp.where((jq == ALIBI_LANE) | (jq == ALIBI_LANE + 3), s_hi,
                       jnp.where((jq == ALIBI_LANE + 1) | (jq == ALIBI_LANE + 4), s_mid, s_lo))
        sl = jnp.where((jq >= ALIBI_LANE) & (jq < ALIBI_LANE + 6), sl, 0.0)
        slope_lanes.append(sl)
        qaug_ref[hrows(h), 0:dh] = q_ref[:, hcols(h)]
        qaug_ref[hrows(h), dh:2 * dh] = sl.astype(bf16)

    def key_side(nrows, off_256, off_low, neg_blocks=None):
        jj = lax.broadcasted_iota(jnp.int32, (nrows, LANES), 1)
        x = jnp.where((jj >= ALIBI_LANE) & (jj < ALIBI_LANE + 3), off_256,
                      jnp.where((jj >= ALIBI_LANE + 3) & (jj < ALIBI_LANE + 6), off_low, 0.0))
        if neg_blocks is not None:
            x = jnp.where(neg_blocks == jj, NEG_BIG, x)
        return x.astype(bf16)

    ncp = kc_ref.shape[0]
    n_id = lax.broadcasted_iota(jnp.int32, (1, ncp), 1)
    end_pos = n_id * CMP_STRIDE + (CMP_BLOCK - 1)
    mask_c = (t >= end_pos) & (n_id < n_cmp)
    end_col = lax.broadcasted_iota(jnp.int32, (ncp, LANES), 0) * CMP_STRIDE + (CMP_BLOCK - 1)
    kc_side = key_side(ncp, (((end_col >> 8) << 8) - t0).astype(f32), (end_col & 255).astype(f32))
    sa_ref[:, 0:ncp] = _dot_nt(qaug_ref[...], jnp.concatenate([kc_ref[...], kc_side], axis=1))
    psum = jnp.zeros((tq, ncp), f32)
    for h in range(hpg):
        sb = jnp.where(mask_c, sa_ref[hrows(h), 0:ncp], NEG_BIG)
        mx = jnp.maximum(jnp.max(sb, axis=-1, keepdims=True), M_INIT)
        e = jnp.exp2(sb - mx)
        p = e * (1.0 / jnp.maximum(jnp.sum(e, axis=-1, keepdims=True), 1e-30))
        psum = psum + p
        p_ref[hrows(h), 0:ncp] = p.astype(bf16)
    oc = _dot(p_ref[:, 0:ncp], vc_ref[...])
    for h in range(hpg):
        out_ref[:, hcols(h)] = gate(h, 0) * oc[hrows(h), :]

    n_i = lax.broadcasted_iota(jnp.int32, (ncp, LANES), 0)
    j_i = lax.broadcasted_iota(jnp.int32, (ncp, LANES), 1)
    ov = (jnp.minimum(n_i * CMP_STRIDE + CMP_BLOCK, (j_i + 1) * SEL_BLOCK)
          - jnp.maximum(n_i * CMP_STRIDE, j_i * SEL_BLOCK))
    ov = jnp.where((n_i < n_cmp) & (j_i < n_blocks), jnp.maximum(ov, 0), 0)
    sel_map = (ov.astype(f32) * (1.0 / CMP_STRIDE)).astype(bf16)
    p_hi = psum.astype(bf16)
    rem = psum - p_hi.astype(f32)
    p_mid = rem.astype(bf16)
    p_lo = (rem - p_mid.astype(f32)).astype(bf16)
    imp = _dot(p_lo, sel_map) + _dot(p_mid, sel_map) + _dot(p_hi, sel_map)

    j = lax.broadcasted_iota(jnp.int32, (tq, LANES), 1)
    cur = t >> SEL_SHIFT
    valid = (j * SEL_BLOCK <= t) & (j < n_blocks)
    forced = (j == 0) | (j == cur) | (j == cur - 1)
    sel = forced & valid
    score = jnp.where(valid & jnp.logical_not(forced), imp, -1.0)
    for _ in range(N_SELECT - 3):
        top = jnp.max(score, axis=-1, keepdims=True)
        hit = (score == top) & (score >= 0.0)
        sel = sel | hit
        score = jnp.where(hit, -1.0, score)
    not_sel = jnp.where(sel, 0.0, 1.0)
    for h in range(hpg):
        qaug_ref[hrows(h), dh:2 * dh] = jnp.where(jq < n_blocks, not_sel, slope_lanes[h]).astype(bf16)

    rr = lax.broadcasted_iota(jnp.int32, (tq, tk), 0)
    cc = lax.broadcasted_iota(jnp.int32, (tq, tk), 1)

    def init_stats():
        m_ref[...] = jnp.full(m_ref.shape, M_INIT, f32)
        l_ref[...] = jnp.zeros(l_ref.shape, f32)
        acc_ref[...] = jnp.zeros(acc_ref.shape, f32)

    def tile_step(k0, width, k_ref, v_ref, block_mask, mask):
        krow = lax.broadcasted_iota(jnp.int32, (width, LANES), 0)
        side = key_side(width, (k0 - t0 + ((krow >> 8) << 8)).astype(f32), (krow & 255).astype(f32),
                        ((k0 + krow) >> SEL_SHIFT) if block_mask else None)
        sa_ref[:, 0:width] = _dot_nt(qaug_ref[...],
                                     jnp.concatenate([k_ref[pl.ds(k0, width), :], side], axis=1))
        for h in range(hpg):
            s = sa_ref[hrows(h), 0:width]
            if mask is not None:
                s = jnp.where((rr >= cc) if mask == "causal" else (rr < cc), s, NEG_BIG)
                sa_ref[hrows(h), 0:width] = s
            m_old = m_ref[hrows(h), :]
            m_new = jnp.maximum(m_old, jnp.max(s, axis=-1, keepdims=True))
            alpha_ref[hrows(h), :] = jnp.exp2(m_old - m_new)
            m_ref[hrows(h), :] = m_new
        for h in range(hpg):
            m_rep = jnp.concatenate([m_ref[hrows(h), :]] * (width // LANES), axis=1)
            p_ref[hrows(h), 0:width] = jnp.exp2(sa_ref[hrows(h), 0:width] - m_rep).astype(bf16)
        ones_blk = jnp.ones((width, LANES), bf16)
        pv = _dot(p_ref[:, 0:width], jnp.concatenate([v_ref[pl.ds(k0, width), :], ones_blk], axis=1))
        alpha = alpha_ref[...]
        acc_ref[...] = acc_ref[...] * alpha + pv[:, 0:dh]
        l_ref[...] = l_ref[...] * alpha + pv[:, dh:2 * dh]

    def finish(c):
        for h in range(hpg):
            o = acc_ref[hrows(h), :] / jnp.maximum(l_ref[hrows(h), :], 1e-30)
            out_ref[:, hcols(h)] += gate(h, c) * o

    def tile_start(kt):
        return pl.multiple_of(kt * tk, tk)

    init_stats()

    def sel_body(kt, carry):
        tile_step(tile_start(kt), tk, ks_ref, vs_ref, True, None)
        return carry

    lax.fori_loop(0, qi, sel_body, 0)
    tile_step(tile_start(qi), tk, ks_ref, vs_ref, True, "causal")
    finish(1)

    init_stats()
    n_back = WINDOW // tk

    @pl.when(qi >= n_back)
    def _():
        tile_step(tile_start(qi - n_back), tk, kw_ref, vw_ref, False, "anti")

    for back in range(n_back - 1, 0, -1):
        @pl.when(qi >= back)
        def _(back=back):
            tile_step(tile_start(qi - back), tk, kw_ref, vw_ref, False, None)

    tile_step(tile_start(qi), tk, kw_ref, vw_ref, False, "causal")
    finish(2)

    o_ref[...] = out_ref[...].astype(o_ref.dtype)


def nsa_attention(q, gate, kvc, kv, batch, seq):
    nt, d = q.shape
    g, dh = NSA_KV_GROUPS, NSA_HEAD_DIM
    heads = d // dh
    hpg = heads // g
    tq = 256
    nq = seq // tq
    n_cmp = (seq - CMP_BLOCK) // CMP_STRIDE + 1
    n_blocks = seq // SEL_BLOCK
    slots = kvc.shape[2]
    assert seq % tq == 0 and WINDOW % tq == 0 and slots == tq
    assert 3 < n_blocks <= ALIBI_LANE and seq <= 256 * 256
    slopes = jnp.exp2(-8.0 * jnp.arange(1, heads + 1, dtype=f32) / heads) * LOG2E
    kv3 = kv.reshape(batch, seq, kv.shape[1])
    rows = hpg * tq

    def kv_spec(part):
        return pl.BlockSpec((None, seq, dh), lambda b, gg, qi: (b, 0, part * g + gg))

    return pl.pallas_call(
        functools.partial(_nsa_kernel, tq=tq, hpg=hpg, n_cmp=n_cmp, n_blocks=n_blocks),
        out_shape=jax.ShapeDtypeStruct((nt, d), bf16),
        grid=(batch, g, nq),
        in_specs=[pl.BlockSpec(memory_space=pltpu.SMEM),
                  pl.BlockSpec((tq, hpg * dh), lambda b, gg, qi: (b * nq + qi, gg)),
                  pl.BlockSpec((tq, LANES), lambda b, gg, qi: (b * nq + qi, gg)),
                  pl.BlockSpec((None, None, slots, dh), lambda b, gg, qi: (b, gg, 0, 0)),
                  pl.BlockSpec((None, None, slots, dh), lambda b, gg, qi: (b, g + gg, 0, 0)),
                  kv_spec(0), kv_spec(1), kv_spec(2), kv_spec(3)],
        out_specs=pl.BlockSpec((tq, hpg * dh), lambda b, gg, qi: (b * nq + qi, gg)),
        scratch_shapes=[pltpu.VMEM((rows, 2 * dh), bf16),
                        pltpu.VMEM((rows, tq), f32),
                        pltpu.VMEM((rows, tq), bf16),
                        pltpu.VMEM((rows, LANES), f32),
                        pltpu.VMEM((rows, LANES), f32),
                        pltpu.VMEM((rows, LANES), f32),
                        pltpu.VMEM((rows, dh), f32),
                        pltpu.VMEM((tq, hpg * dh), f32)],
        compiler_params=_params(("parallel", "parallel", "arbitrary")),
        name="nsa_attention",
    )(slopes, q, gate, kvc, kvc, kv3, kv3, kv3, kv3)


def _gate_weights(w_gate, heads):
    d = w_gate.shape[0]
    g = NSA_KV_GROUPS
    per = (heads // g) * 3
    wg = w_gate.reshape(d, g, per)
    wg = jnp.pad(wg, ((0, 0), (0, 0), (0, LANES - per)))
    return wg.reshape(1, d, g * LANES).astype(bf16)


def kernel(x, attn_norm, ffn_norm, w_ret_in, ret_gn_gain, w_ret_out, kv_norm, w_kv, cmp_pos, cmp_w1,
           cmp_w2, w_nsa_q, w_nsa_out, w_ffn_in, conv_w, conv_b, w_ffn_out, final_norm):
    batch, seq, d = x.shape
    depth = attn_norm.shape[0]
    n_a = w_ret_in.shape[0]
    heads = d // NSA_HEAD_DIM
    q_scale = NSA_HEAD_DIM ** -0.5 * LOG2E
    xs = x.reshape(batch * seq, d)
    w_ret_out_bf = w_ret_out.astype(bf16)
    w_nsa_out_bf = w_nsa_out.astype(bf16)
    w_ffn_out_bf = w_ffn_out.astype(bf16)

    def mixer_gains(layer):
        if layer == n_a:
            return jnp.stack([attn_norm[layer], kv_norm])
        return attn_norm[layer:layer + 1]

    normed, ssq = rmsnorm(xs, mixer_gains(0), bf16), None
    kvc = kv = None
    for layer in range(depth):
        ffn_gain = ffn_norm[layer:layer + 1]
        if layer < n_a:
            proj = matmul_w32(normed[0], w_ret_in, layer, 4 * d, ssq=ssq, name="ret_in")
            y = retention_core(proj, ret_gn_gain[layer], batch, seq)
            xs, normed, ssq = matmul(y, w_ret_out_bf, layer, res=xs, out_dtype=f32, next_gains=ffn_gain,
                                     name="ret_out")
        else:
            b = layer - n_a
            if layer == n_a:
                n_cmp_cols = 2 * NSA_KV_GROUPS * NSA_HEAD_DIM
                kv_cmp = matmul_w32(normed[1], w_kv[None], 0, n_cmp_cols, ssq=ssq, name="kv_cmp_proj")
                kv = matmul_w32(normed[1], w_kv[None], 0, w_kv.shape[1] - n_cmp_cols, col0=n_cmp_cols,
                                ssq=ssq, name="kv_proj")
                kvc = compress_kv(kv_cmp, cmp_pos, cmp_w1, cmp_w2, batch, seq)
            q = matmul_w32(normed[0], w_nsa_q, b, d, ssq=ssq, out_scale=q_scale, name="nsa_q")
            gate = matmul(normed[0], _gate_weights(w_nsa_q[b][:, d:], heads), out_dtype=f32, sigmoid=True,
                          ssq=ssq, name="nsa_gate")
            o = nsa_attention(q, gate, kvc, kv, batch, seq)
            xs, normed, ssq = matmul(o, w_nsa_out_bf, b, res=xs, out_dtype=f32, next_gains=ffn_gain,
                                     name="nsa_out")
        act = ffn_in(normed[0], ssq, w_ffn_in, conv_w, conv_b, layer, seq)
        if layer + 1 < depth:
            xs, normed, ssq = matmul(act, w_ffn_out_bf, layer, res=xs, out_dtype=f32,
                                     next_gains=mixer_gains(layer + 1), name="ffn_out")
        else:
            xs = matmul(act, w_ffn_out_bf, layer, res=xs, out_dtype=f32, name="ffn_out")
    (out,) = rmsnorm(xs, final_norm.reshape(1, d), f32)
    return out.reshape(batch, seq, d)
```

```python
import functools
import math

import jax
import jax.numpy as jnp
from jax import lax
from jax.experimental import pallas as pl
from jax.experimental.pallas import tpu as pltpu

f32 = jnp.float32
bf16 = jnp.bfloat16

RET_HEAD_DIM = 256
RET_CHUNK = 128
NSA_HEAD_DIM = 128
NSA_KV_GROUPS = 4
CMP_BLOCK = 32
CMP_STRIDE = 16
SEL_BLOCK = 64
SEL_SHIFT = 6
N_SELECT = 16
WINDOW = 512
CONV_WIDTH = 3
NORM_EPS = 1e-6
GN_EPS = 1e-5
NEG_BIG = -1e30
M_INIT = -1e29
LOG2E = math.log2(math.e)
LANES = 128
ALIBI_LANE = LANES - 8

V7X_VMEM_BYTES = 64 * 1024 * 1024
VMEM_LIMIT = V7X_VMEM_BYTES - 8 * 1024 * 1024
BF16_SUBLANES = 16


def _pick(n, candidates):
    for c in candidates:
        if c <= n and n % c == 0:
            return c
    return n


def _params(sem):
    return pltpu.CompilerParams(dimension_semantics=sem, vmem_limit_bytes=VMEM_LIMIT)


def _dot(a, b):
    return jnp.dot(a, b, preferred_element_type=f32)


def _dot_nt(a, b):
    return lax.dot_general(a, b, (((1,), (1,)), ((), ())), preferred_element_type=f32)


def _rmsnorm_kernel(x_ref, g_ref, *o_refs):
    x = x_ref[...]
    y = x * lax.rsqrt(jnp.mean(x * x, axis=-1, keepdims=True) + NORM_EPS)
    for n, o_ref in enumerate(o_refs):
        o_ref[...] = (y * g_ref[n:n + 1, :]).astype(o_ref.dtype)


def rmsnorm(x, gains, out_dtype):
    m, d = x.shape
    n = gains.shape[0]
    tm = _pick(m, (256, 128, 64, 32, 16, 8))
    outs = pl.pallas_call(
        _rmsnorm_kernel,
        out_shape=[jax.ShapeDtypeStruct((m, d), out_dtype)] * n,
        grid=(m // tm,),
        in_specs=[pl.BlockSpec((tm, d), lambda i: (i, 0)),
                  pl.BlockSpec((n, d), lambda i: (0, 0))],
        out_specs=[pl.BlockSpec((tm, d), lambda i: (i, 0))] * n,
        compiler_params=_params(("parallel",)),
        name="rmsnorm",
    )(x, gains)
    return outs


def _stream_stats_kernel(x_ref, xb_ref, ssq_ref):
    x = x_ref[...]
    xb_ref[...] = x.astype(bf16)
    ssq_ref[...] = jnp.broadcast_to(jnp.sum(x * x, axis=-1, keepdims=True), ssq_ref.shape)


def stream_stats(x):
    m, d = x.shape
    tm = _pick(m, (256, 128, 64, 32, 16))
    return pl.pallas_call(
        _stream_stats_kernel,
        out_shape=[jax.ShapeDtypeStruct((m, d), bf16), jax.ShapeDtypeStruct((m, LANES), f32)],
        grid=(m // tm,),
        in_specs=[pl.BlockSpec((tm, d), lambda i: (i, 0))],
        out_specs=[pl.BlockSpec((tm, d), lambda i: (i, 0)), pl.BlockSpec((tm, LANES), lambda i: (i, 0))],
        compiler_params=_params(("parallel",)),
        name="stream_stats",
    )(x)


def _row_scale(ssq_ref, d, width):
    rs = lax.rsqrt(ssq_ref[...] * (1.0 / d) + NORM_EPS)
    return jnp.concatenate([rs] * (width // LANES), axis=1)


def _matmul_kernel(*refs, has_res, sigmoid, norm_d, emit_norm):
    it = iter(refs)
    a_ref, w_ref = next(it), next(it)
    res_ref = next(it) if has_res else None
    ssq_in_ref = next(it) if norm_d else None
    o_ref = next(it)
    xb_ref, ssq_out_ref = (next(it), next(it)) if emit_norm else (None, None)
    j = pl.program_id(1)

    r = _dot(a_ref[...], w_ref[...])
    if norm_d:
        r = r * _row_scale(ssq_in_ref, norm_d, r.shape[1])
    if has_res:
        r = r + res_ref[...]
    if sigmoid:
        r = jax.nn.sigmoid(r)
    o_ref[...] = r.astype(o_ref.dtype)
    if emit_norm:
        xb_ref[...] = r.astype(bf16)
        part = jnp.sum(r * r, axis=-1, keepdims=True)

        @pl.when(j == 0)
        def _():
            ssq_out_ref[...] = jnp.broadcast_to(part, ssq_out_ref.shape)

        @pl.when(j > 0)
        def _():
            ssq_out_ref[...] += part


def matmul(a, w, layer=0, res=None, out_dtype=bf16, sigmoid=False, ssq=None, emit_norm=False, name="matmul"):
    m, kdim = a.shape
    n = w.shape[2]
    tm = _pick(m, (1024, 512, 256, 128) if kdim <= 4096 else (512, 256, 128))
    tn = _pick(n, (512, 256, 128))
    in_specs = [pl.BlockSpec((tm, kdim), lambda i, j: (i, 0)),
                pl.BlockSpec((None, kdim, tn), lambda i, j: (layer, 0, j))]
    args = [a, w]
    if res is not None:
        in_specs.append(pl.BlockSpec((tm, tn), lambda i, j: (i, j)))
        args.append(res)
    if ssq is not None:
        in_specs.append(pl.BlockSpec((tm, LANES), lambda i, j: (i, 0)))
        args.append(ssq)
    out_shape = [jax.ShapeDtypeStruct((m, n), out_dtype)]
    out_specs = [pl.BlockSpec((tm, tn), lambda i, j: (i, j))]
    if emit_norm:
        out_shape += [jax.ShapeDtypeStruct((m, n), bf16), jax.ShapeDtypeStruct((m, LANES), f32)]
        out_specs += [pl.BlockSpec((tm, tn), lambda i, j: (i, j)), pl.BlockSpec((tm, LANES), lambda i, j: (i, 0))]
    outs = pl.pallas_call(
        functools.partial(_matmul_kernel, has_res=res is not None, sigmoid=sigmoid,
                          norm_d=kdim if ssq is not None else 0, emit_norm=emit_norm),
        out_shape=out_shape,
        grid=(m // tm, n // tn),
        in_specs=in_specs,
        out_specs=out_specs,
        compiler_params=_params(("parallel", "arbitrary")),
        name=name,
    )(*args)
    return tuple(outs) if emit_norm else outs[0]


def _gain_cols(gain_ref, width):
    return jnp.concatenate([gain_ref[...]] * (width // LANES), axis=1)


def _matmul_w32_kernel(*refs, norm_d, out_scale):
    it = iter(refs)
    a_ref, w_ref = next(it), next(it)
    ssq_ref, gain_ref = (next(it), next(it)) if norm_d else (None, None)
    o_ref, wbf_ref = next(it), next(it)

    @pl.when(pl.program_id(1) == 0)
    def _():
        w = w_ref[...]
        if norm_d:
            w = w * _gain_cols(gain_ref, w.shape[1])
        wbf_ref[...] = w.astype(bf16)

    r = _dot(a_ref[...], wbf_ref[...])
    if norm_d:
        r = r * _row_scale(ssq_ref, norm_d, r.shape[1])
    if out_scale is not None:
        r = r * out_scale
    o_ref[...] = r.astype(o_ref.dtype)


def matmul_w32(a, w, layer, n_out, col0=0, norm=None, out_dtype=bf16, out_scale=None, name="matmul_w32"):
    m, kdim = a.shape
    tm = _pick(m, (1024, 512, 256, 128))
    tn = _pick(n_out, (512, 256, 128))
    assert col0 % tn == 0
    j0 = col0 // tn
    in_specs = [pl.BlockSpec((tm, kdim), lambda j, i: (i, 0)),
                pl.BlockSpec((None, kdim, tn), lambda j, i: (layer, 0, j0 + j))]
    args = [a, w]
    if norm is not None:
        in_specs += [pl.BlockSpec((tm, LANES), lambda j, i: (i, 0)),
                     pl.BlockSpec((kdim, LANES), lambda j, i: (0, 0))]
        args += list(norm)
    return pl.pallas_call(
        functools.partial(_matmul_w32_kernel, norm_d=kdim if norm is not None else 0, out_scale=out_scale),
        out_shape=jax.ShapeDtypeStruct((m, n_out), out_dtype),
        grid=(n_out // tn, m // tm),
        in_specs=in_specs,
        out_specs=pl.BlockSpec((tm, tn), lambda j, i: (i, j)),
        scratch_shapes=[pltpu.VMEM((kdim, tn), bf16)],
        compiler_params=_params(("parallel", "arbitrary")),
        name=name,
    )(*args)


def _retention_kernel(q_ref, k_ref, v_ref, g_ref, gain_ref, o_ref, state_ref, mask_ref, qd_ref, kd_ref,
                      *, heads):
    b = pl.program_id(0)
    c = pl.program_id(1)
    C, dh = RET_CHUNK, RET_HEAD_DIM
    log_g = [math.log1p(-(2.0 ** (-5 - h))) for h in range(heads)]

    @pl.when((b == 0) & (c == 0))
    def _():
        diff = (lax.broadcasted_iota(jnp.int32, (C, C), 0)
                - lax.broadcasted_iota(jnp.int32, (C, C), 1)).astype(f32)
        pos = lax.broadcasted_iota(jnp.int32, (C, LANES), 0).astype(f32)
        for h in range(heads):
            mask_ref[h] = jnp.where(diff >= 0, jnp.exp(jnp.maximum(diff, 0.0) * log_g[h]), 0.0)
            qd_ref[h] = jnp.exp((pos + 1.0) * log_g[h])
            kd_ref[h] = jnp.exp((C - 1.0 - pos) * log_g[h])

    @pl.when(c == 0)
    def _():
        state_ref[...] = jnp.zeros_like(state_ref)

    def lanes2(x):
        return jnp.concatenate([x] * (dh // LANES), axis=1)

    for h in range(heads):
        sl = slice(h * dh, (h + 1) * dh)
        q = q_ref[:, sl]
        ks = k_ref[:, sl].astype(f32) * (dh ** -0.5)
        v = v_ref[:, sl]
        state = state_ref[h]
        qk = _dot_nt(q, ks.astype(bf16)) * mask_ref[h]
        inner = _dot(qk.astype(bf16), v)
        cross = _dot(q, state.astype(bf16)) * lanes2(qd_ref[h])
        kd = (ks * lanes2(kd_ref[h])).astype(bf16)
        upd = lax.dot_general(kd, v, (((0,), (0,)), ((), ())), preferred_element_type=f32)
        state_ref[h] = state * math.exp(C * log_g[h]) + upd

        y = inner + cross
        mu = jnp.mean(y, axis=-1, keepdims=True)
        yc = y - mu
        var = jnp.mean(yc * yc, axis=-1, keepdims=True)
        yn = yc * lax.rsqrt(var + GN_EPS) * gain_ref[:, sl]
        gt = g_ref[:, sl].astype(f32)
        o_ref[:, sl] = (gt * jax.nn.sigmoid(gt) * yn).astype(o_ref.dtype)


def retention_core(proj, gn_gain, batch, seq):
    nt, d4 = proj.shape
    d = d4 // 4
    heads = d // RET_HEAD_DIM
    nc = seq // RET_CHUNK
    blk = (RET_CHUNK, d)

    def spec(off):
        return pl.BlockSpec(blk, lambda b, c: (b * nc + c, off))

    return pl.pallas_call(
        functools.partial(_retention_kernel, heads=heads),
        out_shape=jax.ShapeDtypeStruct((nt, d), bf16),
        grid=(batch, nc),
        in_specs=[spec(0), spec(1), spec(2), spec(3),
                  pl.BlockSpec((1, d), lambda b, c: (0, 0))],
        out_specs=pl.BlockSpec(blk, lambda b, c: (b * nc + c, 0)),
        scratch_shapes=[pltpu.VMEM((heads, RET_HEAD_DIM, RET_HEAD_DIM), f32),
                        pltpu.VMEM((heads, RET_CHUNK, RET_CHUNK), f32),
                        pltpu.VMEM((heads, RET_CHUNK, LANES), f32),
                        pltpu.VMEM((heads, RET_CHUNK, LANES), f32)],
        compiler_params=_params(("arbitrary", "arbitrary")),
        name="retention",
    )(proj, proj, proj, proj, gn_gain.reshape(1, d))


def _ffn_in_kernel(h_ref, halo_ref, ssq_ref, ssq_halo_ref, gain_ref, wa_ref, wu_ref, cw_ref, cb_ref, o_ref,
                   wab_ref, wub_ref, *, tm, seq):
    i = pl.program_id(1)

    @pl.when(i == 0)
    def _():
        g = _gain_cols(gain_ref, wa_ref.shape[1])
        wab_ref[...] = (wa_ref[...] * g).astype(bf16)
        wub_ref[...] = (wu_ref[...] * g).astype(bf16)

    h = h_ref[...]
    wa = wab_ref[...]
    d, tn = wa.shape
    rs = _row_scale(ssq_ref, d, tn)
    a = _dot(h, wa) * rs
    u = _dot(h, wub_ref[...]) * rs
    ah = _dot(halo_ref[...], wa) * _row_scale(ssq_halo_ref, d, tn)
    seq_start = (i * tm) % seq == 0
    ah = jnp.where(seq_start, 0.0, ah)
    p1 = ah[BF16_SUBLANES - 1:BF16_SUBLANES, :]
    p2 = ah[BF16_SUBLANES - 2:BF16_SUBLANES - 1, :]
    a1 = pltpu.roll(a, 1, 0)
    a2 = pltpu.roll(a, 2, 0)
    row = lax.broadcasted_iota(jnp.int32, (8, a.shape[1]), 0)
    top1 = jnp.where(row == 0, p1, a1[0:8, :])
    top2 = jnp.where(row == 0, p2, jnp.where(row == 1, p1, a2[0:8, :]))
    a1 = jnp.concatenate([top1, a1[8:, :]], axis=0)
    a2 = jnp.concatenate([top2, a2[8:, :]], axis=0)
    cw = cw_ref[...]
    conv = a2 * cw[0:1, :] + a1 * cw[1:2, :] + a * cw[2:3, :] + cb_ref[...]
    o_ref[...] = (conv * jax.nn.sigmoid(conv) * u).astype(o_ref.dtype)


def ffn_in(xb, ssq, gain, w_in, conv_w, conv_b, layer, seq):
    m, d = xb.shape
    f = w_in.shape[2] // 2
    tm = _pick(seq, (1024, 512, 256, 128))
    tn = _pick(f, (256, 128))
    nf = f // tn
    hb = tm // BF16_SUBLANES

    def halo(j, i):
        return (jnp.maximum(i * hb - 1, 0), 0)

    return pl.pallas_call(
        functools.partial(_ffn_in_kernel, tm=tm, seq=seq),
        out_shape=jax.ShapeDtypeStruct((m, f), bf16),
        grid=(nf, m // tm),
        in_specs=[pl.BlockSpec((tm, d), lambda j, i: (i, 0)),
                  pl.BlockSpec((BF16_SUBLANES, d), halo),
                  pl.BlockSpec((tm, LANES), lambda j, i: (i, 0)),
                  pl.BlockSpec((BF16_SUBLANES, LANES), halo),
                  pl.BlockSpec((d, LANES), lambda j, i: (0, 0)),
                  pl.BlockSpec((None, d, tn), lambda j, i: (layer, 0, j)),
                  pl.BlockSpec((None, d, tn), lambda j, i: (layer, 0, nf + j)),
                  pl.BlockSpec((None, CONV_WIDTH, tn), lambda j, i: (layer, 0, j)),
                  pl.BlockSpec((None, 1, tn), lambda j, i: (layer, 0, j))],
        out_specs=pl.BlockSpec((tm, tn), lambda j, i: (i, j)),
        scratch_shapes=[pltpu.VMEM((d, tn), bf16), pltpu.VMEM((d, tn), bf16)],
        compiler_params=_params(("parallel", "arbitrary")),
        name="ffn_in",
    )(xb, xb, ssq, ssq, gain, w_in, w_in, conv_w, conv_b.reshape(conv_b.shape[0], 1, f))


def _compress_kernel(t_ref, pos_ref, w1cat_ref, w1_ref, w2_ref, o_ref, *, n_cmp, n_r, g, ncols):
    dh = NSA_HEAD_DIM
    slots = o_ref.shape[1]
    rid = lax.broadcasted_iota(jnp.int32, (slots, dh), 0)
    for part in range(2):
        pos8 = jnp.broadcast_to(pos_ref[part], (8, pos_ref.shape[2])).astype(bf16)
        pos_term = _dot(pos8, w1_ref[part])[0:1, :]
        for gg in range(g):
            pg = part * g + gg
            acc = jnp.zeros((slots, 2 * dh), f32)
            for r in range(n_r):
                col = (r * ncols + pg) * dh
                acc = acc + _dot(t_ref[:, col:col + dh], w1cat_ref[part, r])
            pre = acc[:, 0:dh] + pltpu.roll(acc[:, dh:2 * dh], slots - 1, 0) + pos_term
            pre = jnp.where(rid < n_cmp, pre, 0.0)
            act = pre * jax.nn.sigmoid(pre)
            o_ref[pg] = _dot(act.astype(bf16), w2_ref[part]).astype(o_ref.dtype)


def compress_kv(kv, cmp_pos, cmp_w1, cmp_w2, batch, seq):
    g, dh = NSA_KV_GROUPS, NSA_HEAD_DIM
    ncols = kv.shape[1] // dh
    n_r = CMP_STRIDE
    slots = seq // CMP_STRIDE
    n_cmp = (seq - CMP_BLOCK) // CMP_STRIDE + 1
    halves = CMP_BLOCK // CMP_STRIDE
    assert halves == 2 and ncols == 2 * g
    kv_v = kv.reshape(batch, slots, n_r * ncols * dh)
    pos = cmp_pos.reshape(2, 1, CMP_BLOCK * dh)
    w1 = cmp_w1.astype(bf16)
    w1cat = w1.reshape(2, halves, n_r, dh, dh).transpose(0, 2, 3, 1, 4).reshape(2, n_r, dh, halves * dh)
    w2 = cmp_w2.astype(bf16)

    def whole(x):
        return pl.BlockSpec(x.shape, lambda b: (0,) * x.ndim)

    return pl.pallas_call(
        functools.partial(_compress_kernel, n_cmp=n_cmp, n_r=n_r, g=g, ncols=ncols),
        out_shape=jax.ShapeDtypeStruct((batch, 2 * g, slots, dh), bf16),
        grid=(batch,),
        in_specs=[pl.BlockSpec((None, slots, n_r * ncols * dh), lambda b: (b, 0, 0)),
                  whole(pos), whole(w1cat), whole(w1), whole(w2)],
        out_specs=pl.BlockSpec((None, 2 * g, slots, dh), lambda b: (b, 0, 0, 0)),
        compiler_params=_params(("parallel",)),
        name="compress_kv",
    )(kv_v, pos, w1cat, w1, w2)


def _nsa_kernel(slopes_ref, q_ref, gate_ref, kc_ref, vc_ref, ks_ref, vs_ref, kw_ref, vw_ref,
                o_ref, qaug_ref, sa_ref, p_ref, m_ref, l_ref, alpha_ref, acc_ref, out_ref,
                *, tq, hpg, n_cmp, n_blocks):
    g = pl.program_id(1)
    qi = pl.program_id(2)
    dh = NSA_HEAD_DIM
    tk = tq
    t0 = qi * tq
    t = t0 + lax.broadcasted_iota(jnp.int32, (tq, 1), 0)

    def hrows(h):
        return slice(h * tq, (h + 1) * tq)

    def hcols(h):
        return slice(h * dh, (h + 1) * dh)

    def gate(h, c):
        return gate_ref[:, h * 3 + c:h * 3 + c + 1]

    def slope(h):
        return slopes_ref[g * hpg + h]

    jq = lax.broadcasted_iota(jnp.int32, (tq, LANES), 1)
    slope_lanes = []
    for h in range(hpg):
        sv = jnp.full((tq, LANES), slope(h), f32)
        s_hi = sv.astype(bf16).astype(f32)
        s_mid = (sv - s_hi).astype(bf16).astype(f32)
        s_lo = sv - s_hi - s_mid
        sl = jnp.where((jq == ALIBI_LANE) | (jq == ALIBI_LANE + 3), s_hi,
                       jnp.where((jq == ALIBI_LANE + 1) | (jq == ALIBI_LANE + 4), s_mid, s_lo))
        sl = jnp.where((jq >= ALIBI_LANE) & (jq < ALIBI_LANE + 6), sl, 0.0)
        slope_lanes.append(sl)
        qaug_ref[hrows(h), 0:dh] = q_ref[:, hcols(h)]
        qaug_ref[hrows(h), dh:2 * dh] = sl.astype(bf16)

    def key_side(nrows, off_256, off_low, neg_blocks=None):
        jj = lax.broadcasted_iota(jnp.int32, (nrows, LANES), 1)
        x = jnp.where((jj >= ALIBI_LANE) & (jj < ALIBI_LANE + 3), off_256,
                      jnp.where((jj >= ALIBI_LANE + 3) & (jj < ALIBI_LANE + 6), off_low, 0.0))
        if neg_blocks is not None:
            x = jnp.where(neg_blocks == jj, NEG_BIG, x)
        return x.astype(bf16)

    ncp = kc_ref.shape[0]
    n_id = lax.broadcasted_iota(jnp.int32, (1, ncp), 1)
    end_pos = n_id * CMP_STRIDE + (CMP_BLOCK - 1)
    mask_c = (t >= end_pos) & (n_id < n_cmp)
    end_col = lax.broadcasted_iota(jnp.int32, (ncp, LANES), 0) * CMP_STRIDE + (CMP_BLOCK - 1)
    kc_side = key_side(ncp, (((end_col >> 8) << 8) - t0).astype(f32), (end_col & 255).astype(f32))
    sa_ref[:, 0:ncp] = _dot_nt(qaug_ref[...], jnp.concatenate([kc_ref[...], kc_side], axis=1))
    psum = jnp.zeros((tq, ncp), f32)
    for h in range(hpg):
        sb = jnp.where(mask_c, sa_ref[hrows(h), 0:ncp], NEG_BIG)
        mx = jnp.maximum(jnp.max(sb, axis=-1, keepdims=True), M_INIT)
        e = jnp.exp2(sb - mx)
        p = e * (1.0 / jnp.maximum(jnp.sum(e, axis=-1, keepdims=True), 1e-30))
        psum = psum + p
        p_ref[hrows(h), 0:ncp] = p.astype(bf16)
    oc = _dot(p_ref[:, 0:ncp], vc_ref[...])
    for h in range(hpg):
        out_ref[:, hcols(h)] = gate(h, 0) * oc[hrows(h), :]

    n_i = lax.broadcasted_iota(jnp.int32, (ncp, LANES), 0)
    j_i = lax.broadcasted_iota(jnp.int32, (ncp, LANES), 1)
    ov = (jnp.minimum(n_i * CMP_STRIDE + CMP_BLOCK, (j_i + 1) * SEL_BLOCK)
          - jnp.maximum(n_i * CMP_STRIDE, j_i * SEL_BLOCK))
    ov = jnp.where((n_i < n_cmp) & (j_i < n_blocks), jnp.maximum(ov, 0), 0)
    sel_map = (ov.astype(f32) * (1.0 / CMP_STRIDE)).astype(bf16)
    p_hi = psum.astype(bf16)
    rem = psum - p_hi.astype(f32)
    p_mid = rem.astype(bf16)
    p_lo = (rem - p_mid.astype(f32)).astype(bf16)
    imp = _dot(p_lo, sel_map) + _dot(p_mid, sel_map) + _dot(p_hi, sel_map)

    j = lax.broadcasted_iota(jnp.int32, (tq, LANES), 1)
    cur = t >> SEL_SHIFT
    valid = (j * SEL_BLOCK <= t) & (j < n_blocks)
    forced = (j == 0) | (j == cur) | (j == cur - 1)
    sel = forced & valid
    score = jnp.where(valid & jnp.logical_not(forced), imp, -1.0)
    for _ in range(N_SELECT - 3):
        top = jnp.max(score, axis=-1, keepdims=True)
        hit = (score == top) & (score >= 0.0)
        sel = sel | hit
        score = jnp.where(hit, -1.0, score)
    not_sel = jnp.where(sel, 0.0, 1.0)
    for h in range(hpg):
        qaug_ref[hrows(h), dh:2 * dh] = jnp.where(jq < n_blocks, not_sel, slope_lanes[h]).astype(bf16)

    rr = lax.broadcasted_iota(jnp.int32, (tq, tk), 0)
    cc = lax.broadcasted_iota(jnp.int32, (tq, tk), 1)

    def init_stats():
        m_ref[...] = jnp.full(m_ref.shape, M_INIT, f32)
        l_ref[...] = jnp.zeros(l_ref.shape, f32)
        acc_ref[...] = jnp.zeros(acc_ref.shape, f32)

    def tile_step(k0, width, k_ref, v_ref, block_mask, mask):
        krow = lax.broadcasted_iota(jnp.int32, (width, LANES), 0)
        side = key_side(width, (k0 - t0 + ((krow >> 8) << 8)).astype(f32), (krow & 255).astype(f32),
                        ((k0 + krow) >> SEL_SHIFT) if block_mask else None)
        sa_ref[:, 0:width] = _dot_nt(qaug_ref[...],
                                     jnp.concatenate([k_ref[pl.ds(k0, width), :], side], axis=1))
        for h in range(hpg):
            s = sa_ref[hrows(h), 0:width]
            if mask is not None:
                s = jnp.where((rr >= cc) if mask == "causal" else (rr < cc), s, NEG_BIG)
                sa_ref[hrows(h), 0:width] = s
            m_old = m_ref[hrows(h), :]
            m_new = jnp.maximum(m_old, jnp.max(s, axis=-1, keepdims=True))
            alpha_ref[hrows(h), :] = jnp.exp2(m_old - m_new)
            m_ref[hrows(h), :] = m_new
        for h in range(hpg):
            m_rep = jnp.concatenate([m_ref[hrows(h), :]] * (width // LANES), axis=1)
            p_ref[hrows(h), 0:width] = jnp.exp2(sa_ref[hrows(h), 0:width] - m_rep).astype(bf16)
        ones_blk = jnp.ones((width, LANES), bf16)
        pv = _dot(p_ref[:, 0:width], jnp.concatenate([v_ref[pl.ds(k0, width), :], ones_blk], axis=1))
        alpha = alpha_ref[...]
        acc_ref[...] = acc_ref[...] * alpha + pv[:, 0:dh]
        l_ref[...] = l_ref[...] * alpha + pv[:, dh:2 * dh]

    def finish(c):
        for h in range(hpg):
            o = acc_ref[hrows(h), :] / jnp.maximum(l_ref[hrows(h), :], 1e-30)
            out_ref[:, hcols(h)] += gate(h, c) * o

    def tile_start(kt):
        return pl.multiple_of(kt * tk, tk)

    init_stats()

    def sel_body(kt, carry):
        tile_step(tile_start(kt), tk, ks_ref, vs_ref, True, None)
        return carry

    lax.fori_loop(0, qi, sel_body, 0)
    tile_step(tile_start(qi), tk, ks_ref, vs_ref, True, "causal")
    finish(1)

    init_stats()
    n_back = WINDOW // tk

    @pl.when(qi >= n_back)
    def _():
        tile_step(tile_start(qi - n_back), tk, kw_ref, vw_ref, False, "anti")

    for back in range(n_back - 1, 0, -1):
        @pl.when(qi >= back)
        def _(back=back):
            tile_step(tile_start(qi - back), tk, kw_ref, vw_ref, False, None)

    tile_step(tile_start(qi), tk, kw_ref, vw_ref, False, "causal")
    finish(2)

    o_ref[...] = out_ref[...].astype(o_ref.dtype)


def nsa_attention(q, gate, kvc, kv, batch, seq):
    nt, d = q.shape
    g, dh = NSA_KV_GROUPS, NSA_HEAD_DIM
    heads = d // dh
    hpg = heads // g
    tq = 256
    nq = seq // tq
    n_cmp = (seq - CMP_BLOCK) // CMP_STRIDE + 1
    n_blocks = seq // SEL_BLOCK
    slots = kvc.shape[2]
    assert seq % tq == 0 and WINDOW % tq == 0 and slots == tq
    assert 3 < n_blocks <= ALIBI_LANE and seq <= 256 * 256
    slopes = jnp.exp2(-8.0 * jnp.arange(1, heads + 1, dtype=f32) / heads) * LOG2E
    kv3 = kv.reshape(batch, seq, kv.shape[1])
    rows = hpg * tq

    def kv_spec(part):
        return pl.BlockSpec((None, seq, dh), lambda b, gg, qi: (b, 0, part * g + gg))

    return pl.pallas_call(
        functools.partial(_nsa_kernel, tq=tq, hpg=hpg, n_cmp=n_cmp, n_blocks=n_blocks),
        out_shape=jax.ShapeDtypeStruct((nt, d), bf16),
        grid=(batch, g, nq),
        in_specs=[pl.BlockSpec(memory_space=pltpu.SMEM),
                  pl.BlockSpec((tq, hpg * dh), lambda b, gg, qi: (b * nq + qi, gg)),
                  pl.BlockSpec((tq, LANES), lambda b, gg, qi: (b * nq + qi, gg)),
                  pl.BlockSpec((None, None, slots, dh), lambda b, gg, qi: (b, gg, 0, 0)),
                  pl.BlockSpec((None, None, slots, dh), lambda b, gg, qi: (b, g + gg, 0, 0)),
                  kv_spec(0), kv_spec(1), kv_spec(2), kv_spec(3)],
        out_specs=pl.BlockSpec((tq, hpg * dh), lambda b, gg, qi: (b * nq + qi, gg)),
        scratch_shapes=[pltpu.VMEM((rows, 2 * dh), bf16),
                        pltpu.VMEM((rows, tq), f32),
                        pltpu.VMEM((rows, tq), bf16),
                        pltpu.VMEM((rows, LANES), f32),
                        pltpu.VMEM((rows, LANES), f32),
                        pltpu.VMEM((rows, LANES), f32),
                        pltpu.VMEM((rows, dh), f32),
                        pltpu.VMEM((tq, hpg * dh), f32)],
        compiler_params=_params(("parallel", "parallel", "arbitrary")),
        name="nsa_attention",
    )(slopes, q, gate, kvc, kvc, kv3, kv3, kv3, kv3)


def _gate_weights(w_gate, heads):
    d = w_gate.shape[0]
    g = NSA_KV_GROUPS
    per = (heads // g) * 3
    wg = w_gate.reshape(d, g, per)
    wg = jnp.pad(wg, ((0, 0), (0, 0), (0, LANES - per)))
    return wg.reshape(1, d, g * LANES).astype(bf16)


def kernel(x, attn_norm, ffn_norm, w_ret_in, ret_gn_gain, w_ret_out, kv_norm, w_kv, cmp_pos, cmp_w1,
           cmp_w2, w_nsa_q, w_nsa_out, w_ffn_in, conv_w, conv_b, w_ffn_out, final_norm):
    batch, seq, d = x.shape
    depth = attn_norm.shape[0]
    n_a = w_ret_in.shape[0]
    heads = d // NSA_HEAD_DIM
    q_scale = NSA_HEAD_DIM ** -0.5 * LOG2E
    xs = x.reshape(batch * seq, d)
    w_ret_out_bf = w_ret_out.astype(bf16)
    w_nsa_out_bf = w_nsa_out.astype(bf16)
    w_ffn_out_bf = w_ffn_out.astype(bf16)

    def lane_rep(gain):
        return jnp.broadcast_to(gain[:, None], (d, LANES))

    xb, ssq = stream_stats(xs)
    kvc = kv = None
    for layer in range(depth):
        mix_norm = (ssq, lane_rep(attn_norm[layer]))
        if layer < n_a:
            proj = matmul_w32(xb, w_ret_in, layer, 4 * d, norm=mix_norm, name="ret_in")
            y = retention_core(proj, ret_gn_gain[layer], batch, seq)
            xs, xb, ssq = matmul(y, w_ret_out_bf, layer, res=xs, out_dtype=f32, emit_norm=True, name="ret_out")
        else:
            b = layer - n_a
            if layer == n_a:
                n_cmp_cols = 2 * NSA_KV_GROUPS * NSA_HEAD_DIM
                kv_norm_in = (ssq, lane_rep(kv_norm))
                kv_cmp = matmul_w32(xb, w_kv[None], 0, n_cmp_cols, norm=kv_norm_in, name="kv_cmp_proj")
                kv = matmul_w32(xb, w_kv[None], 0, w_kv.shape[1] - n_cmp_cols, col0=n_cmp_cols,
                                norm=kv_norm_in, name="kv_proj")
                kvc = compress_kv(kv_cmp, cmp_pos, cmp_w1, cmp_w2, batch, seq)
            q = matmul_w32(xb, w_nsa_q, b, d, norm=mix_norm, out_scale=q_scale, name="nsa_q")
            w_gate = w_nsa_q[b][:, d:] * attn_norm[layer][:, None]
            gate = matmul(xb, _gate_weights(w_gate, heads), out_dtype=f32, sigmoid=True, ssq=ssq,
                          name="nsa_gate")
            o = nsa_attention(q, gate, kvc, kv, batch, seq)
            xs, xb, ssq = matmul(o, w_nsa_out_bf, b, res=xs, out_dtype=f32, emit_norm=True, name="nsa_out")
        act = ffn_in(xb, ssq, lane_rep(ffn_norm[layer]), w_ffn_in, conv_w, conv_b, layer, seq)
        if layer + 1 < depth:
            xs, xb, ssq = matmul(act, w_ffn_out_bf, layer, res=xs, out_dtype=f32, emit_norm=True,
                                 name="ffn_out")
        else:
            xs = matmul(act, w_ffn_out_bf, layer, res=xs, out_dtype=f32, name="ffn_out")
    (out,) = rmsnorm(xs, final_norm.reshape(1, d), f32)
    return out.reshape(batch, seq, d)
```

```python
import functools
import math

import jax
import jax.numpy as jnp
from jax import lax
from jax.experimental import pallas as pl
from jax.experimental.pallas import tpu as pltpu

f32 = jnp.float32
bf16 = jnp.bfloat16

RET_HEAD_DIM = 256
RET_CHUNK = 128
NSA_HEAD_DIM = 128
NSA_KV_GROUPS = 4
CMP_BLOCK = 32
CMP_STRIDE = 16
SEL_BLOCK = 64
SEL_SHIFT = 6
N_SELECT = 16
WINDOW = 512
CONV_WIDTH = 3
NORM_EPS = 1e-6
GN_EPS = 1e-5
NEG_BIG = -1e30
M_INIT = -1e29
LOG2E = math.log2(math.e)
LANES = 128
ALIBI_LANE = LANES - 8

V7X_VMEM_BYTES = 64 * 1024 * 1024
VMEM_LIMIT = V7X_VMEM_BYTES - 8 * 1024 * 1024
BF16_SUBLANES = 16


def _pick(n, candidates):
    for c in candidates:
        if c <= n and n % c == 0:
            return c
    return n


def _params(sem):
    return pltpu.CompilerParams(dimension_semantics=sem, vmem_limit_bytes=VMEM_LIMIT)


def _dot(a, b):
    return jnp.dot(a, b, preferred_element_type=f32)


def _dot_nt(a, b):
    return lax.dot_general(a, b, (((1,), (1,)), ((), ())), preferred_element_type=f32)


def _rmsnorm_kernel(x_ref, g_ref, *o_refs):
    x = x_ref[...]
    y = x * lax.rsqrt(jnp.mean(x * x, axis=-1, keepdims=True) + NORM_EPS)
    for n, o_ref in enumerate(o_refs):
        o_ref[...] = (y * g_ref[n:n + 1, :]).astype(o_ref.dtype)


def rmsnorm(x, gains, out_dtype):
    m, d = x.shape
    n = gains.shape[0]
    tm = _pick(m, (256, 128, 64, 32, 16, 8))
    outs = pl.pallas_call(
        _rmsnorm_kernel,
        out_shape=[jax.ShapeDtypeStruct((m, d), out_dtype)] * n,
        grid=(m // tm,),
        in_specs=[pl.BlockSpec((tm, d), lambda i: (i, 0)),
                  pl.BlockSpec((n, d), lambda i: (0, 0))],
        out_specs=[pl.BlockSpec((tm, d), lambda i: (i, 0))] * n,
        compiler_params=_params(("parallel",)),
        name="rmsnorm",
    )(x, gains)
    return outs


def _row_scale(ssq_ref, d, width):
    rs = lax.rsqrt(ssq_ref[...] * (1.0 / d) + NORM_EPS)
    return jnp.concatenate([rs] * (width // LANES), axis=1)


def _matmul_kernel(*refs, has_res, sigmoid, norm_d, n_gains):
    it = iter(refs)
    a_ref, w_ref = next(it), next(it)
    res_ref = next(it) if has_res else None
    ssq_in_ref = next(it) if norm_d else None
    gains_ref = next(it) if n_gains else None
    o_ref = next(it)
    xg_refs = [next(it) for _ in range(n_gains)]
    ssq_out_ref = next(it) if n_gains else None
    j = pl.program_id(1)

    r = _dot(a_ref[...], w_ref[...])
    if norm_d:
        r = r * _row_scale(ssq_in_ref, norm_d, r.shape[1])
    if has_res:
        r = r + res_ref[...]
    if sigmoid:
        r = jax.nn.sigmoid(r)
    o_ref[...] = r.astype(o_ref.dtype)
    if n_gains:
        for n, xg_ref in enumerate(xg_refs):
            xg_ref[...] = (r * gains_ref[n:n + 1, :]).astype(bf16)
        part = jnp.sum(r * r, axis=-1, keepdims=True)

        @pl.when(j == 0)
        def _():
            ssq_out_ref[...] = jnp.broadcast_to(part, ssq_out_ref.shape)

        @pl.when(j > 0)
        def _():
            ssq_out_ref[...] += part


def matmul(a, w, layer=0, res=None, out_dtype=bf16, sigmoid=False, ssq=None, next_gains=None, name="matmul"):
    m, kdim = a.shape
    n = w.shape[2]
    tm = _pick(m, (1024, 512, 256, 128) if kdim <= 4096 else (512, 256, 128))
    tn = _pick(n, (512, 256, 128))
    n_gains = 0 if next_gains is None else next_gains.shape[0]
    in_specs = [pl.BlockSpec((tm, kdim), lambda i, j: (i, 0)),
                pl.BlockSpec((None, kdim, tn), lambda i, j: (layer, 0, j))]
    args = [a, w]
    if res is not None:
        in_specs.append(pl.BlockSpec((tm, tn), lambda i, j: (i, j)))
        args.append(res)
    if ssq is not None:
        in_specs.append(pl.BlockSpec((tm, LANES), lambda i, j: (i, 0)))
        args.append(ssq)
    out_shape = [jax.ShapeDtypeStruct((m, n), out_dtype)]
    out_specs = [pl.BlockSpec((tm, tn), lambda i, j: (i, j))]
    if n_gains:
        in_specs.append(pl.BlockSpec((n_gains, tn), lambda i, j: (0, j)))
        args.append(next_gains)
        out_shape += [jax.ShapeDtypeStruct((m, n), bf16)] * n_gains + [jax.ShapeDtypeStruct((m, LANES), f32)]
        out_specs += [pl.BlockSpec((tm, tn), lambda i, j: (i, j))] * n_gains
        out_specs += [pl.BlockSpec((tm, LANES), lambda i, j: (i, 0))]
    outs = pl.pallas_call(
        functools.partial(_matmul_kernel, has_res=res is not None, sigmoid=sigmoid,
                          norm_d=kdim if ssq is not None else 0, n_gains=n_gains),
        out_shape=out_shape,
        grid=(m // tm, n // tn),
        in_specs=in_specs,
        out_specs=out_specs,
        compiler_params=_params(("parallel", "arbitrary")),
        name=name,
    )(*args)
    if n_gains:
        return outs[0], list(outs[1:1 + n_gains]), outs[-1]
    return outs[0]


def _matmul_w32_kernel(*refs, norm_d, out_scale):
    it = iter(refs)
    a_ref, w_ref = next(it), next(it)
    ssq_ref = next(it) if norm_d else None
    o_ref, wbf_ref = next(it), next(it)

    @pl.when(pl.program_id(1) == 0)
    def _():
        wbf_ref[...] = w_ref[...].astype(bf16)

    r = _dot(a_ref[...], wbf_ref[...])
    if norm_d:
        r = r * _row_scale(ssq_ref, norm_d, r.shape[1])
    if out_scale is not None:
        r = r * out_scale
    o_ref[...] = r.astype(o_ref.dtype)


def matmul_w32(a, w, layer, n_out, col0=0, ssq=None, out_dtype=bf16, out_scale=None, name="matmul_w32"):
    m, kdim = a.shape
    tm = _pick(m, (1024, 512, 256, 128))
    tn = _pick(n_out, (512, 256, 128))
    assert col0 % tn == 0
    j0 = col0 // tn
    in_specs = [pl.BlockSpec((tm, kdim), lambda j, i: (i, 0)),
                pl.BlockSpec((None, kdim, tn), lambda j, i: (layer, 0, j0 + j))]
    args = [a, w]
    if ssq is not None:
        in_specs.append(pl.BlockSpec((tm, LANES), lambda j, i: (i, 0)))
        args.append(ssq)
    return pl.pallas_call(
        functools.partial(_matmul_w32_kernel, norm_d=kdim if ssq is not None else 0, out_scale=out_scale),
        out_shape=jax.ShapeDtypeStruct((m, n_out), out_dtype),
        grid=(n_out // tn, m // tm),
        in_specs=in_specs,
        out_specs=pl.BlockSpec((tm, tn), lambda j, i: (i, j)),
        scratch_shapes=[pltpu.VMEM((kdim, tn), bf16)],
        compiler_params=_params(("parallel", "arbitrary")),
        name=name,
    )(*args)


def _retention_kernel(q_ref, k_ref, v_ref, g_ref, gain_ref, o_ref, state_ref, mask_ref, qd_ref, kd_ref,
                      *, heads):
    b = pl.program_id(0)
    c = pl.program_id(1)
    C, dh = RET_CHUNK, RET_HEAD_DIM
    log_g = [math.log1p(-(2.0 ** (-5 - h))) for h in range(heads)]

    @pl.when((b == 0) & (c == 0))
    def _():
        diff = (lax.broadcasted_iota(jnp.int32, (C, C), 0)
                - lax.broadcasted_iota(jnp.int32, (C, C), 1)).astype(f32)
        pos = lax.broadcasted_iota(jnp.int32, (C, LANES), 0).astype(f32)
        for h in range(heads):
            mask_ref[h] = jnp.where(diff >= 0, jnp.exp(jnp.maximum(diff, 0.0) * log_g[h]), 0.0)
            qd_ref[h] = jnp.exp((pos + 1.0) * log_g[h])
            kd_ref[h] = jnp.exp((C - 1.0 - pos) * log_g[h])

    @pl.when(c == 0)
    def _():
        state_ref[...] = jnp.zeros_like(state_ref)

    def lanes2(x):
        return jnp.concatenate([x] * (dh // LANES), axis=1)

    for h in range(heads):
        sl = slice(h * dh, (h + 1) * dh)
        q = q_ref[:, sl]
        ks = k_ref[:, sl].astype(f32) * (dh ** -0.5)
        v = v_ref[:, sl]
        state = state_ref[h]
        qk = _dot_nt(q, ks.astype(bf16)) * mask_ref[h]
        inner = _dot(qk.astype(bf16), v)
        cross = _dot(q, state.astype(bf16)) * lanes2(qd_ref[h])
        kd = (ks * lanes2(kd_ref[h])).astype(bf16)
        upd = lax.dot_general(kd, v, (((0,), (0,)), ((), ())), preferred_element_type=f32)
        state_ref[h] = state * math.exp(C * log_g[h]) + upd

        y = inner + cross
        mu = jnp.mean(y, axis=-1, keepdims=True)
        yc = y - mu
        var = jnp.mean(yc * yc, axis=-1, keepdims=True)
        yn = yc * lax.rsqrt(var + GN_EPS) * gain_ref[:, sl]
        gt = g_ref[:, sl].astype(f32)
        o_ref[:, sl] = (gt * jax.nn.sigmoid(gt) * yn).astype(o_ref.dtype)


def retention_core(proj, gn_gain, batch, seq):
    nt, d4 = proj.shape
    d = d4 // 4
    heads = d // RET_HEAD_DIM
    nc = seq // RET_CHUNK
    blk = (RET_CHUNK, d)

    def spec(off):
        return pl.BlockSpec(blk, lambda b, c: (b * nc + c, off))

    return pl.pallas_call(
        functools.partial(_retention_kernel, heads=heads),
        out_shape=jax.ShapeDtypeStruct((nt, d), bf16),
        grid=(batch, nc),
        in_specs=[spec(0), spec(1), spec(2), spec(3),
                  pl.BlockSpec((1, d), lambda b, c: (0, 0))],
        out_specs=pl.BlockSpec(blk, lambda b, c: (b * nc + c, 0)),
        scratch_shapes=[pltpu.VMEM((heads, RET_HEAD_DIM, RET_HEAD_DIM), f32),
                        pltpu.VMEM((heads, RET_CHUNK, RET_CHUNK), f32),
                        pltpu.VMEM((heads, RET_CHUNK, LANES), f32),
                        pltpu.VMEM((heads, RET_CHUNK, LANES), f32)],
        compiler_params=_params(("arbitrary", "arbitrary")),
        name="retention",
    )(proj, proj, proj, proj, gn_gain.reshape(1, d))


def _ffn_in_kernel(h_ref, halo_ref, ssq_ref, ssq_halo_ref, wa_ref, wu_ref, cw_ref, cb_ref, o_ref,
                   wab_ref, wub_ref, *, tm, seq):
    i = pl.program_id(1)

    @pl.when(i == 0)
    def _():
        wab_ref[...] = wa_ref[...].astype(bf16)
        wub_ref[...] = wu_ref[...].astype(bf16)

    h = h_ref[...]
    wa = wab_ref[...]
    d, tn = wa.shape
    rs = _row_scale(ssq_ref, d, tn)
    a = _dot(h, wa) * rs
    u = _dot(h, wub_ref[...]) * rs
    ah = _dot(halo_ref[...], wa) * _row_scale(ssq_halo_ref, d, tn)
    seq_start = (i * tm) % seq == 0
    ah = jnp.where(seq_start, 0.0, ah)
    p1 = ah[BF16_SUBLANES - 1:BF16_SUBLANES, :]
    p2 = ah[BF16_SUBLANES - 2:BF16_SUBLANES - 1, :]
    a1 = pltpu.roll(a, 1, 0)
    a2 = pltpu.roll(a, 2, 0)
    row = lax.broadcasted_iota(jnp.int32, (8, a.shape[1]), 0)
    top1 = jnp.where(row == 0, p1, a1[0:8, :])
    top2 = jnp.where(row == 0, p2, jnp.where(row == 1, p1, a2[0:8, :]))
    a1 = jnp.concatenate([top1, a1[8:, :]], axis=0)
    a2 = jnp.concatenate([top2, a2[8:, :]], axis=0)
    cw = cw_ref[...]
    conv = a2 * cw[0:1, :] + a1 * cw[1:2, :] + a * cw[2:3, :] + cb_ref[...]
    o_ref[...] = (conv * jax.nn.sigmoid(conv) * u).astype(o_ref.dtype)


def ffn_in(xg, ssq, w_in, conv_w, conv_b, layer, seq):
    m, d = xg.shape
    f = w_in.shape[2] // 2
    tm = _pick(seq, (1024, 512, 256, 128))
    tn = _pick(f, (256, 128))
    nf = f // tn
    hb = tm // BF16_SUBLANES

    def halo(j, i):
        return (jnp.maximum(i * hb - 1, 0), 0)

    return pl.pallas_call(
        functools.partial(_ffn_in_kernel, tm=tm, seq=seq),
        out_shape=jax.ShapeDtypeStruct((m, f), bf16),
        grid=(nf, m // tm),
        in_specs=[pl.BlockSpec((tm, d), lambda j, i: (i, 0)),
                  pl.BlockSpec((BF16_SUBLANES, d), halo),
                  pl.BlockSpec((tm, LANES), lambda j, i: (i, 0)),
                  pl.BlockSpec((BF16_SUBLANES, LANES), halo),
                  pl.BlockSpec((None, d, tn), lambda j, i: (layer, 0, j)),
                  pl.BlockSpec((None, d, tn), lambda j, i: (layer, 0, nf + j)),
                  pl.BlockSpec((None, CONV_WIDTH, tn), lambda j, i: (layer, 0, j)),
                  pl.BlockSpec((None, 1, tn), lambda j, i: (layer, 0, j))],
        out_specs=pl.BlockSpec((tm, tn), lambda j, i: (i, j)),
        scratch_shapes=[pltpu.VMEM((d, tn), bf16), pltpu.VMEM((d, tn), bf16)],
        compiler_params=_params(("parallel", "arbitrary")),
        name="ffn_in",
    )(xg, xg, ssq, ssq, w_in, w_in, conv_w, conv_b.reshape(conv_b.shape[0], 1, f))


def _compress_kernel(t_ref, pos_ref, w1cat_ref, w1_ref, w2_ref, o_ref, *, n_cmp, n_r, g, ncols):
    dh = NSA_HEAD_DIM
    slots = o_ref.shape[1]
    rid = lax.broadcasted_iota(jnp.int32, (slots, dh), 0)
    for part in range(2):
        pos8 = jnp.broadcast_to(pos_ref[part], (8, pos_ref.shape[2])).astype(bf16)
        pos_term = _dot(pos8, w1_ref[part])[0:1, :]
        for gg in range(g):
            pg = part * g + gg
            acc = jnp.zeros((slots, 2 * dh), f32)
            for r in range(n_r):
                col = (r * ncols + pg) * dh
                acc = acc + _dot(t_ref[:, col:col + dh], w1cat_ref[part, r])
            pre = acc[:, 0:dh] + pltpu.roll(acc[:, dh:2 * dh], slots - 1, 0) + pos_term
            pre = jnp.where(rid < n_cmp, pre, 0.0)
            act = pre * jax.nn.sigmoid(pre)
            o_ref[pg] = _dot(act.astype(bf16), w2_ref[part]).astype(o_ref.dtype)


def compress_kv(kv, cmp_pos, cmp_w1, cmp_w2, batch, seq):
    g, dh = NSA_KV_GROUPS, NSA_HEAD_DIM
    ncols = kv.shape[1] // dh
    n_r = CMP_STRIDE
    slots = seq // CMP_STRIDE
    n_cmp = (seq - CMP_BLOCK) // CMP_STRIDE + 1
    halves = CMP_BLOCK // CMP_STRIDE
    assert halves == 2 and ncols == 2 * g
    kv_v = kv.reshape(batch, slots, n_r * ncols * dh)
    pos = cmp_pos.reshape(2, 1, CMP_BLOCK * dh)
    w1 = cmp_w1.astype(bf16)
    w1cat = w1.reshape(2, halves, n_r, dh, dh).transpose(0, 2, 3, 1, 4).reshape(2, n_r, dh, halves * dh)
    w2 = cmp_w2.astype(bf16)

    def whole(x):
        return pl.BlockSpec(x.shape, lambda b: (0,) * x.ndim)

    return pl.pallas_call(
        functools.partial(_compress_kernel, n_cmp=n_cmp, n_r=n_r, g=g, ncols=ncols),
        out_shape=jax.ShapeDtypeStruct((batch, 2 * g, slots, dh), bf16),
        grid=(batch,),
        in_specs=[pl.BlockSpec((None, slots, n_r * ncols * dh), lambda b: (b, 0, 0)),
                  whole(pos), whole(w1cat), whole(w1), whole(w2)],
        out_specs=pl.BlockSpec((None, 2 * g, slots, dh), lambda b: (b, 0, 0, 0)),
        compiler_params=_params(("parallel",)),
        name="compress_kv",
    )(kv_v, pos, w1cat, w1, w2)


def _nsa_kernel(slopes_ref, q_ref, gate_ref, kc_ref, vc_ref, ks_ref, vs_ref, kw_ref, vw_ref,
                o_ref, qaug_ref, sa_ref, p_ref, m_ref, l_ref, alpha_ref, acc_ref, out_ref,
                *, tq, hpg, n_cmp, n_blocks):
    g = pl.program_id(1)
    qi = pl.program_id(2)
    dh = NSA_HEAD_DIM
    tk = tq
    t0 = qi * tq
    t = t0 + lax.broadcasted_iota(jnp.int32, (tq, 1), 0)

    def hrows(h):
        return slice(h * tq, (h + 1) * tq)

    def hcols(h):
        return slice(h * dh, (h + 1) * dh)

    def gate(h, c):
        return gate_ref[:, h * 3 + c:h * 3 + c + 1]

    def slope(h):
        return slopes_ref[g * hpg + h]

    jq = lax.broadcasted_iota(jnp.int32, (tq, LANES), 1)
    slope_lanes = []
    for h in range(hpg):
        sv = jnp.full((tq, LANES), slope(h), f32)
        s_hi = sv.astype(bf16).astype(f32)
        s_mid = (sv - s_hi).astype(bf16).astype(f32)
        s_lo = sv - s_hi - s_mid
        sl = jnp.where((jq == ALIBI_LANE) | (jq == ALIBI_LANE + 3), s_hi,
                       jnp.where((jq == ALIBI_LANE + 1) | (jq == ALIBI_LANE + 4), s_mid, s_lo))
        sl = jnp.where((jq >= ALIBI_LANE) & (jq < ALIBI_LANE + 6), sl, 0.0)
        slope_lanes.append(sl)
        qaug_ref[hrows(h), 0:dh] = q_ref[:, hcols(h)]
        qaug_ref[hrows(h), dh:2 * dh] = sl.astype(bf16)

    def key_side(nrows, off_256, off_low, neg_blocks=None):
        jj = lax.broadcasted_iota(jnp.int32, (nrows, LANES), 1)
        x = jnp.where((jj >= ALIBI_LANE) & (jj < ALIBI_LANE + 3), off_256,
                      jnp.where((jj >= ALIBI_LANE + 3) & (jj < ALIBI_LANE + 6), off_low, 0.0))
        if neg_blocks is not None:
            x = jnp.where(neg_blocks == jj, NEG_BIG, x)
        return x.astype(bf16)

    ncp = kc_ref.shape[0]
    n_id = lax.broadcasted_iota(jnp.int32, (1, ncp), 1)
    end_pos = n_id * CMP_STRIDE + (CMP_BLOCK - 1)
    mask_c = (t >= end_pos) & (n_id < n_cmp)
    end_col = lax.broadcasted_iota(jnp.int32, (ncp, LANES), 0) * CMP_STRIDE + (CMP_BLOCK - 1)
    kc_side = key_side(ncp, (((end_col >> 8) << 8) - t0).astype(f32), (end_col & 255).astype(f32))
    sa_ref[:, 0:ncp] = _dot_nt(qaug_ref[...], jnp.concatenate([kc_ref[...], kc_side], axis=1))
    psum = jnp.zeros((tq, ncp), f32)
    for h in range(hpg):
        sb = jnp.where(mask_c, sa_ref[hrows(h), 0:ncp], NEG_BIG)
        mx = jnp.maximum(jnp.max(sb, axis=-1, keepdims=True), M_INIT)
        e = jnp.exp2(sb - mx)
        p = e * (1.0 / jnp.maximum(jnp.sum(e, axis=-1, keepdims=True), 1e-30))
        psum = psum + p
        p_ref[hrows(h), 0:ncp] = p.astype(bf16)
    oc = _dot(p_ref[:, 0:ncp], vc_ref[...])
    for h in range(hpg):
        out_ref[:, hcols(h)] = gate(h, 0) * oc[hrows(h), :]

    j_i = lax.broadcasted_iota(jnp.int32, (LANES, ncp), 0)
    n_i = lax.broadcasted_iota(jnp.int32, (LANES, ncp), 1)
    ov = (jnp.minimum(n_i * CMP_STRIDE + CMP_BLOCK, (j_i + 1) * SEL_BLOCK)
          - jnp.maximum(n_i * CMP_STRIDE, j_i * SEL_BLOCK))
    ov = jnp.where((n_i < n_cmp) & (j_i < n_blocks), jnp.maximum(ov, 0), 0)
    sel_map_t = (ov.astype(f32) * (1.0 / CMP_STRIDE)).astype(bf16)
    p_hi = psum.astype(bf16)
    rem = psum - p_hi.astype(f32)
    p_mid = rem.astype(bf16)
    p_lo = (rem - p_mid.astype(f32)).astype(bf16)
    imp = _dot_nt(sel_map_t, p_lo) + _dot_nt(sel_map_t, p_mid) + _dot_nt(sel_map_t, p_hi)

    t_l = t0 + lax.broadcasted_iota(jnp.int32, (1, tq), 1)
    j = lax.broadcasted_iota(jnp.int32, (LANES, tq), 0)
    cur = t_l >> SEL_SHIFT
    valid = (j * SEL_BLOCK <= t_l) & (j < n_blocks)
    forced = (j == 0) | (j == cur) | (j == cur - 1)
    sel = forced & valid
    score = jnp.where(valid & jnp.logical_not(forced), imp, -1.0)
    for _ in range(N_SELECT - 3):
        top = jnp.max(score, axis=0, keepdims=True)
        hit = (score == top) & (score >= 0.0)
        sel = sel | hit
        score = jnp.where(hit, -1.0, score)
    not_sel = jnp.where(sel, 0.0, 1.0).T
    for h in range(hpg):
        qaug_ref[hrows(h), dh:2 * dh] = jnp.where(jq < n_blocks, not_sel, slope_lanes[h]).astype(bf16)

    rr = lax.broadcasted_iota(jnp.int32, (tq, tk), 0)
    cc = lax.broadcasted_iota(jnp.int32, (tq, tk), 1)

    def init_stats():
        m_ref[...] = jnp.full(m_ref.shape, M_INIT, f32)
        l_ref[...] = jnp.zeros(l_ref.shape, f32)
        acc_ref[...] = jnp.zeros(acc_ref.shape, f32)

    def tile_step(k0, width, k_ref, v_ref, block_mask, mask):
        krow = lax.broadcasted_iota(jnp.int32, (width, LANES), 0)
        side = key_side(width, (k0 - t0 + ((krow >> 8) << 8)).astype(f32), (krow & 255).astype(f32),
                        ((k0 + krow) >> SEL_SHIFT) if block_mask else None)
        sa_ref[:, 0:width] = _dot_nt(qaug_ref[...],
                                     jnp.concatenate([k_ref[pl.ds(k0, width), :], side], axis=1))
        for h in range(hpg):
            s = sa_ref[hrows(h), 0:width]
            if mask is not None:
                s = jnp.where((rr >= cc) if mask == "causal" else (rr < cc), s, NEG_BIG)
                sa_ref[hrows(h), 0:width] = s
            m_old = m_ref[hrows(h), :]
            m_new = jnp.maximum(m_old, jnp.max(s, axis=-1, keepdims=True))
            alpha_ref[hrows(h), :] = jnp.exp2(m_old - m_new)
            m_ref[hrows(h), :] = m_new
        for h in range(hpg):
            m_rep = jnp.concatenate([m_ref[hrows(h), :]] * (width // LANES), axis=1)
            p_ref[hrows(h), 0:width] = jnp.exp2(sa_ref[hrows(h), 0:width] - m_rep).astype(bf16)
        ones_blk = jnp.ones((width, LANES), bf16)
        pv = _dot(p_ref[:, 0:width], jnp.concatenate([v_ref[pl.ds(k0, width), :], ones_blk], axis=1))
        alpha = alpha_ref[...]
        acc_ref[...] = acc_ref[...] * alpha + pv[:, 0:dh]
        l_ref[...] = l_ref[...] * alpha + pv[:, dh:2 * dh]

    def finish(c):
        for h in range(hpg):
            o = acc_ref[hrows(h), :] / jnp.maximum(l_ref[hrows(h), :], 1e-30)
            out_ref[:, hcols(h)] += gate(h, c) * o

    def tile_start(kt):
        return pl.multiple_of(kt * tk, tk)

    init_stats()

    def sel_body(kt, carry):
        tile_step(tile_start(kt), tk, ks_ref, vs_ref, True, None)
        return carry

    lax.fori_loop(0, qi, sel_body, 0)
    tile_step(tile_start(qi), tk, ks_ref, vs_ref, True, "causal")
    finish(1)

    init_stats()
    n_back = WINDOW // tk

    @pl.when(qi >= n_back)
    def _():
        tile_step(tile_start(qi - n_back), tk, kw_ref, vw_ref, False, "anti")

    for back in range(n_back - 1, 0, -1):
        @pl.when(qi >= back)
        def _(back=back):
            tile_step(tile_start(qi - back), tk, kw_ref, vw_ref, False, None)

    tile_step(tile_start(qi), tk, kw_ref, vw_ref, False, "causal")
    finish(2)

    o_ref[...] = out_ref[...].astype(o_ref.dtype)


def nsa_attention(q, gate, kvc, kv, batch, seq):
    nt, d = q.shape
    g, dh = NSA_KV_GROUPS, NSA_HEAD_DIM
    heads = d // dh
    hpg = heads // g
    tq = 256
    nq = seq // tq
    n_cmp = (seq - CMP_BLOCK) // CMP_STRIDE + 1
    n_blocks = seq // SEL_BLOCK
    slots = kvc.shape[2]
    assert seq % tq == 0 and WINDOW % tq == 0 and slots == tq
    assert 3 < n_blocks <= ALIBI_LANE and seq <= 256 * 256
    slopes = jnp.exp2(-8.0 * jnp.arange(1, heads + 1, dtype=f32) / heads) * LOG2E
    kv3 = kv.reshape(batch, seq, kv.shape[1])
    rows = hpg * tq

    def kv_spec(part):
        return pl.BlockSpec((None, seq, dh), lambda b, gg, qi: (b, 0, part * g + gg))

    return pl.pallas_call(
        functools.partial(_nsa_kernel, tq=tq, hpg=hpg, n_cmp=n_cmp, n_blocks=n_blocks),
        out_shape=jax.ShapeDtypeStruct((nt, d), bf16),
        grid=(batch, g, nq),
        in_specs=[pl.BlockSpec(memory_space=pltpu.SMEM),
                  pl.BlockSpec((tq, hpg * dh), lambda b, gg, qi: (b * nq + qi, gg)),
                  pl.BlockSpec((tq, LANES), lambda b, gg, qi: (b * nq + qi, gg)),
                  pl.BlockSpec((None, None, slots, dh), lambda b, gg, qi: (b, gg, 0, 0)),
                  pl.BlockSpec((None, None, slots, dh), lambda b, gg, qi: (b, g + gg, 0, 0)),
                  kv_spec(0), kv_spec(1), kv_spec(2), kv_spec(3)],
        out_specs=pl.BlockSpec((tq, hpg * dh), lambda b, gg, qi: (b * nq + qi, gg)),
        scratch_shapes=[pltpu.VMEM((rows, 2 * dh), bf16),
                        pltpu.VMEM((rows, tq), f32),
                        pltpu.VMEM((rows, tq), bf16),
                        pltpu.VMEM((rows, LANES), f32),
                        pltpu.VMEM((rows, LANES), f32),
                        pltpu.VMEM((rows, LANES), f32),
                        pltpu.VMEM((rows, dh), f32),
                        pltpu.VMEM((tq, hpg * dh), f32)],
        compiler_params=_params(("parallel", "parallel", "arbitrary")),
        name="nsa_attention",
    )(slopes, q, gate, kvc, kvc, kv3, kv3, kv3, kv3)


def _gate_weights(w_gate, heads):
    d = w_gate.shape[0]
    g = NSA_KV_GROUPS
    per = (heads // g) * 3
    wg = w_gate.reshape(d, g, per)
    wg = jnp.pad(wg, ((0, 0), (0, 0), (0, LANES - per)))
    return wg.reshape(1, d, g * LANES).astype(bf16)


def kernel(x, attn_norm, ffn_norm, w_ret_in, ret_gn_gain, w_ret_out, kv_norm, w_kv, cmp_pos, cmp_w1,
           cmp_w2, w_nsa_q, w_nsa_out, w_ffn_in, conv_w, conv_b, w_ffn_out, final_norm):
    batch, seq, d = x.shape
    depth = attn_norm.shape[0]
    n_a = w_ret_in.shape[0]
    heads = d // NSA_HEAD_DIM
    q_scale = NSA_HEAD_DIM ** -0.5 * LOG2E
    xs = x.reshape(batch * seq, d)
    w_ret_out_bf = w_ret_out.astype(bf16)
    w_nsa_out_bf = w_nsa_out.astype(bf16)
    w_ffn_out_bf = w_ffn_out.astype(bf16)

    def mixer_gains(layer):
        if layer == n_a:
            return jnp.stack([attn_norm[layer], kv_norm])
        return attn_norm[layer:layer + 1]

    normed, ssq = rmsnorm(xs, mixer_gains(0), bf16), None
    kvc = kv = None
    for layer in range(depth):
        ffn_gain = ffn_norm[layer:layer + 1]
        if layer < n_a:
            proj = matmul_w32(normed[0], w_ret_in, layer, 4 * d, ssq=ssq, name="ret_in")
            y = retention_core(proj, ret_gn_gain[layer], batch, seq)
            xs, normed, ssq = matmul(y, w_ret_out_bf, layer, res=xs, out_dtype=f32, next_gains=ffn_gain,
                                     name="ret_out")
        else:
            b = layer - n_a
            if layer == n_a:
                n_cmp_cols = 2 * NSA_KV_GROUPS * NSA_HEAD_DIM
                kv_cmp = matmul_w32(normed[1], w_kv[None], 0, n_cmp_cols, ssq=ssq, name="kv_cmp_proj")
                kv = matmul_w32(normed[1], w_kv[None], 0, w_kv.shape[1] - n_cmp_cols, col0=n_cmp_cols,
                                ssq=ssq, name="kv_proj")
                kvc = compress_kv(kv_cmp, cmp_pos, cmp_w1, cmp_w2, batch, seq)
            q = matmul_w32(normed[0], w_nsa_q, b, d, ssq=ssq, out_scale=q_scale, name="nsa_q")
            gate = matmul(normed[0], _gate_weights(w_nsa_q[b][:, d:], heads), out_dtype=f32, sigmoid=True,
                          ssq=ssq, name="nsa_gate")
            o = nsa_attention(q, gate, kvc, kv, batch, seq)
            xs, normed, ssq = matmul(o, w_nsa_out_bf, b, res=xs, out_dtype=f32, next_gains=ffn_gain,
                                     name="nsa_out")
        act = ffn_in(normed[0], ssq, w_ffn_in, conv_w, conv_b, layer, seq)
        if layer + 1 < depth:
            xs, normed, ssq = matmul(act, w_ffn_out_bf, layer, res=xs, out_dtype=f32,
                                     next_gains=mixer_gains(layer + 1), name="ffn_out")
        else:
            xs = matmul(act, w_ffn_out_bf, layer, res=xs, out_dtype=f32, name="ffn_out")
    (out,) = rmsnorm(xs, final_norm.reshape(1, d), f32)
    return out.reshape(batch, seq, d)
```
